```python
import math
import jax, jax.numpy as jnp
from jax import lax
import numpy as np

D_MODEL = 1024
BATCH = 4
SEQ = 4096
DEPTH = 4

N_META = 16
A_HEADS = 8
A_HEAD_DIM = 64
A_WIDTH = A_HEADS * A_HEAD_DIM
Q_RANK = 256
IDX_HEADS = 16
IDX_DIM = 64
TOPK_MAX = 256
Q_BLOCK = 128
B_HEADS = 8
B_KEY_DIM = 64
B_VAL_DIM = 64
B_KEY_WIDTH = B_HEADS * B_KEY_DIM
B_VAL_WIDTH = B_HEADS * B_VAL_DIM
CHUNK = 64
POOL_WINDOWS = (2, 4, 8, 16)
POOL_GROUPS = 4
POOL_GROUP_DIM = D_MODEL // POOL_GROUPS
MIX_WIDTH = A_WIDTH + B_VAL_WIDTH
IN_COLS = Q_RANK + 2 * A_WIDTH + IDX_DIM + IDX_HEADS + 2 * B_KEY_WIDTH + 2 * B_VAL_WIDTH
D_FF = -(-8 * D_MODEL // (3 * 256)) * 256
ALPHA = (2 * DEPTH) ** 0.25
BETA = (8 * DEPTH) ** -0.25
N_EVEN = (DEPTH + 1) // 2
N_ODD = DEPTH // 2
LN_EPS = 1e-5

kernel_name = "hybrid_dsa_hgrn2_pool_trunk"


def layer_norm(x, g, b):
    xf = x.astype(jnp.float32)
    mu = jnp.mean(xf, axis=-1, keepdims=True)
    var = jnp.mean(jnp.square(xf - mu), axis=-1, keepdims=True)
    y = (xf - mu) * lax.rsqrt(var + LN_EPS)
    return (y * g.astype(jnp.float32) + b.astype(jnp.float32)).astype(x.dtype)


def rms_norm(x, g):
    xf = x.astype(jnp.float32)
    y = xf * lax.rsqrt(jnp.mean(jnp.square(xf), axis=-1, keepdims=True) + LN_EPS)
    return (y * g.astype(jnp.float32)).astype(x.dtype)


def alibi_slopes(n):
    return 2.0 ** (-8.0 * jnp.arange(1, n + 1, dtype=jnp.float32) / n)


def dsa_attention(q, k, v, q_idx, k_idx, w_idx, topk):
    B, L = k.shape[0], k.shape[1]
    n_blk = -(-L // Q_BLOCK)
    Lq = n_blk * Q_BLOCK
    pad = Lq - L

    def to_blocks(a):
        a = jnp.pad(a, [(0, 0), (0, pad)] + [(0, 0)] * (a.ndim - 2))
        return a.reshape((B, n_blk, Q_BLOCK) + a.shape[2:]).swapaxes(0, 1)

    slopes = alibi_slopes(A_HEADS)
    key_pos = jnp.arange(L)
    scale = A_HEAD_DIM ** -0.5

    def block(args):
        qb, qib, wb, t = args
        s = jnp.einsum('bqhd,bsd->bqhs', qib, k_idx)
        score = jnp.einsum('bqhs,bqh->bqs', jax.nn.relu(s), wb).astype(jnp.float32)
        causal = key_pos[None, :] <= t[:, None]
        score = jnp.where(causal[None], score, -jnp.inf)
        _, idx = lax.top_k(score, topk)
        valid = idx <= t[None, :, None]
        k_sel = jax.vmap(lambda kb, ib: kb[ib])(k, idx)
        v_sel = jax.vmap(lambda vb, ib: vb[ib])(v, idx)
        logits = jnp.einsum('bqhd,bqkhd->bhqk', qb, k_sel).astype(jnp.float32) * scale
        dist = (t[None, :, None] - idx).astype(jnp.float32)
        logits = logits - slopes[None, :, None, None] * dist[:, None]
        logits = jnp.where(valid[:, None], logits, -jnp.inf)
        p = jax.nn.softmax(logits, axis=-1).astype(v.dtype)
        return jnp.einsum('bhqk,bqkhd->bqhd', p, v_sel)

    t_blocks = jnp.arange(Lq).reshape(n_blk, Q_BLOCK)
    out = lax.map(block, (to_blocks(q), to_blocks(q_idx), to_blocks(w_idx), t_blocks))
    out = out.swapaxes(0, 1).reshape(B, Lq, A_HEADS, A_HEAD_DIM)
    return out[:, :L]


def hgrn2_chunkwise(q, logf, k, v):
    B, L, H, DK = q.shape
    DV = v.shape[-1]
    left = (-N_META) % CHUNK
    right = (-(L + left)) % CHUNK
    Lp = L + left + right
    N = Lp // CHUNK

    def chunk(a):
        a = jnp.pad(a, ((0, 0), (left, right), (0, 0), (0, 0)))
        return a.reshape(B, N, CHUNK, H, a.shape[-1]).transpose(1, 0, 3, 2, 4)

    qc, gc, kc, vc = chunk(q), chunk(logf), chunk(k), chunk(v)
    causal = jnp.tril(jnp.ones((CHUNK, CHUNK), dtype=bool))

    def step(S, xs):
        qn, gn, kn, vn = xs
        b = jnp.cumsum(gn, axis=2)
        inter = jnp.einsum('bhtd,bhde->bhte', qn * jnp.exp(b), S)
        diff = b[:, :, :, None, :] - b[:, :, None, :, :]
        decay = jnp.exp(jnp.where(causal[:, :, None], diff, -jnp.inf))
        scores = jnp.einsum('bhtd,bhtsd,bhsd->bhts', qn, decay, kn)
        intra = jnp.einsum('bhts,bhse->bhte', scores, vn)
        b_last = b[:, :, -1:, :]
        S = jnp.exp(b_last[:, :, 0, :])[..., None] * S + jnp.einsum('bhsd,bhse->bhde', kn * jnp.exp(b_last - b), vn)
        return S, inter + intra

    S0 = jnp.zeros((B, H, DK, DV), v.dtype)
    _, out = lax.scan(step, S0, (qc, gc, kc, vc))
    out = out.transpose(1, 0, 3, 2, 4).reshape(B, Lp, H, DV)
    return out[:, left:left + L]


def attn_recurrent_mixer(h, w_in, cq_gain, w_uq, w_uq_idx, kidx_gain, kidx_bias, lb, onorm_gain, w_out, topk):
    B, L, _ = h.shape
    sizes = [Q_RANK, A_WIDTH, A_WIDTH, IDX_DIM, IDX_HEADS, B_KEY_WIDTH, B_KEY_WIDTH, B_VAL_WIDTH, B_VAL_WIDTH]
    splits = np.cumsum(sizes)[:-1].tolist()
    proj = h @ w_in
    c_q, k_a, v_a, k_i, w_i, q_b, z_f, i_b, g_b = jnp.split(proj, splits, axis=-1)
    c_q = rms_norm(c_q, cq_gain)
    q_a = (c_q @ w_uq).reshape(B, L, A_HEADS, A_HEAD_DIM)
    q_i = (c_q @ w_uq_idx).reshape(B, L, IDX_HEADS, IDX_DIM)
    k_i = layer_norm(k_i, kidx_gain, kidx_bias)
    w_i = w_i * (IDX_HEADS ** -0.5 * IDX_DIM ** -0.5)
    o_a = dsa_attention(q_a, k_a.reshape(B, L, A_HEADS, A_HEAD_DIM), v_a.reshape(B, L, A_HEADS, A_HEAD_DIM),
                        q_i, k_i, w_i, topk)
    z_f = z_f.reshape(B, L, B_HEADS, B_KEY_DIM)
    lb = lb.reshape(B_HEADS, B_KEY_DIM)
    logf = jnp.logaddexp(jnp.log(lb), jnp.log1p(-lb) + jax.nn.log_sigmoid(z_f.astype(jnp.float32))).astype(h.dtype)
    k_b = ((1.0 - lb) * jax.nn.sigmoid(-z_f.astype(jnp.float32))).astype(h.dtype)
    o_b = hgrn2_chunkwise(q_b.reshape(B, L, B_HEADS, B_KEY_DIM), logf, k_b,
                          i_b.reshape(B, L, B_HEADS, B_VAL_DIM))
    o_b = rms_norm(o_b, onorm_gain.reshape(B_HEADS, B_VAL_DIM)) * jax.nn.silu(g_b.reshape(B, L, B_HEADS, B_VAL_DIM))
    o = jnp.concatenate([o_a.reshape(B, L, A_WIDTH), o_b.reshape(B, L, B_VAL_WIDTH)], axis=-1)
    return o @ w_out


def multiscale_pool(h, w_pool, scale):
    B, L, D = h.shape
    hf = h.astype(jnp.float32)
    cs = jnp.concatenate([jnp.zeros((B, 1, D), jnp.float32), jnp.cumsum(hf, axis=1)], axis=1)
    pos = jnp.arange(L)
    outs = []
    for g, w in enumerate(POOL_WINDOWS):
        sl = slice(g * POOL_GROUP_DIM, (g + 1) * POOL_GROUP_DIM)
        cs_g = cs[..., sl]
        lo = jnp.maximum(pos + 1 - w, 0)
        cnt = (pos + 1 - lo).astype(jnp.float32)
        mean = (cs_g[:, 1:] - cs_g[:, lo]) / cnt[None, :, None]
        d = (mean - hf[..., sl]).astype(h.dtype)
        outs.append(d @ w_pool[g])
    return jnp.concatenate(outs, axis=-1) * scale


def swiglu(h, w_gate, w_up, w_down):
    return (jax.nn.silu(h @ w_gate) * (h @ w_up)) @ w_down


def setup_inputs(seed: int = 0) -> dict:
    key = jax.random.key(seed)
    ks = jax.random.split(key, 24)
    f32 = jnp.float32

    def nrm(k, shape, s):
        return jax.random.normal(k, shape, f32) * s

    D = D_MODEL
    return {
        "x": nrm(ks[0], (BATCH, SEQ, D), 1.0),
        "meta_tokens": nrm(ks[1], (N_META, D), 1.0),
        "w_in": nrm(ks[2], (N_EVEN, D, IN_COLS), D ** -0.5),
        "cq_gain": 1.0 + nrm(ks[3], (N_EVEN, Q_RANK), 0.02),
        "w_uq": nrm(ks[4], (N_EVEN, Q_RANK, A_WIDTH), Q_RANK ** -0.5),
        "w_uq_idx": nrm(ks[5], (N_EVEN, Q_RANK, IDX_HEADS * IDX_DIM), Q_RANK ** -0.5),
        "kidx_gain": 1.0 + nrm(ks[6], (N_EVEN, IDX_DIM), 0.02),
        "kidx_bias": nrm(ks[7], (N_EVEN, IDX_DIM), 0.02),
        "lb_raw": nrm(ks[8], (N_EVEN, B_KEY_WIDTH), 0.5),
        "onorm_gain": 1.0 + nrm(ks[9], (N_EVEN, B_VAL_WIDTH), 0.02),
        "w_out": nrm(ks[10], (N_EVEN, MIX_WIDTH, D), MIX_WIDTH ** -0.5 * BETA),
        "w_pool": nrm(ks[11], (N_ODD, POOL_GROUPS, POOL_GROUP_DIM, POOL_GROUP_DIM), POOL_GROUP_DIM ** -0.5 * BETA),
        "pool_scale": 1.0 + nrm(ks[12], (N_ODD, D), 0.02),
        "ln_mix_g": 1.0 + nrm(ks[13], (DEPTH, D), 0.02),
        "ln_mix_b": nrm(ks[14], (DEPTH, D), 0.02),
        "w_gate": nrm(ks[15], (DEPTH, D, D_FF), D ** -0.5),
        "w_up": nrm(ks[16], (DEPTH, D, D_FF), D ** -0.5),
        "w_down": nrm(ks[17], (DEPTH, D_FF, D), D_FF ** -0.5 * BETA),
        "ln_ffn_g": 1.0 + nrm(ks[18], (DEPTH, D), 0.02),
        "ln_ffn_b": nrm(ks[19], (DEPTH, D), 0.02),
    }


def reference(x, meta_tokens, w_in, cq_gain, w_uq, w_uq_idx, kidx_gain, kidx_bias, lb_raw, onorm_gain, w_out,
              w_pool, pool_scale, ln_mix_g, ln_mix_b, w_gate, w_up, w_down, ln_ffn_g, ln_ffn_b):
    B, S, D = x.shape
    topk = min(TOPK_MAX, S // 4)
    meta = jnp.broadcast_to(meta_tokens.astype(x.dtype)[None], (B, N_META, D))
    h = jnp.concatenate([meta, x], axis=1)
    lower = jnp.cumsum(jax.nn.softmax(lb_raw.astype(jnp.float32), axis=0), axis=0)
    lower = lower - lower[:1]
    for layer in range(DEPTH):
        j = layer // 2
        if layer % 2 == 0:
            mix = attn_recurrent_mixer(h, w_in[j], cq_gain[j], w_uq[j], w_uq_idx[j], kidx_gain[j], kidx_bias[j],
                                       lower[j], onorm_gain[j], w_out[j], topk)
        else:
            mix = multiscale_pool(h, w_pool[j], pool_scale[j])
        h = layer_norm(ALPHA * h + mix, ln_mix_g[layer], ln_mix_b[layer])
        h = layer_norm(ALPHA * h + swiglu(h, w_gate[layer], w_up[layer], w_down[layer]), ln_ffn_g[layer], ln_ffn_b[layer])
    return h[:, N_META:]
```

```python
import functools

import jax
import jax.numpy as jnp
import numpy as np
from jax import lax
from jax.experimental import pallas as pl
from jax.experimental.pallas import tpu as pltpu

D_MODEL = 1024
DEPTH = 4
N_META = 16
A_HEADS = 8
A_HEAD_DIM = 64
A_WIDTH = A_HEADS * A_HEAD_DIM
Q_RANK = 256
IDX_HEADS = 16
IDX_DIM = 64
TOPK_MAX = 256
B_HEADS = 8
B_KEY_DIM = 64
B_WIDTH = B_HEADS * B_KEY_DIM
POOL_WINDOWS = (2, 4, 8, 16)
POOL_GROUP_DIM = D_MODEL // len(POOL_WINDOWS)
D_FF = -(-8 * D_MODEL // (3 * 256)) * 256
ALPHA = (2 * DEPTH) ** 0.25
LN_EPS = 1e-5

LANES = 128
SUBLANES = 8
MXU_DIM = 256
VMEM_LIMIT = 56 * 1024 * 1024

SEQ_TILE = 256
HG_CHUNK = 64
POOL_HALO = 16

_SEG_A = (0, Q_RANK + 2 * A_WIDTH)
_SEG_K = (_SEG_A[1], _SEG_A[1] + 2 * LANES)
_SEG_B = (_SEG_K[1], _SEG_K[1] + 4 * B_WIDTH)
IN_COLS_PADDED = _SEG_B[1]

INT_MIN = -2 ** 31
NEG_BIG = -1e30

bf16 = jnp.bfloat16
f32 = jnp.float32


def _dot(a, b):
    return jnp.dot(a, b, preferred_element_type=f32)


def _dot_nt(a, b):
    return lax.dot_general(a, b, (((1,), (1,)), ((), ())), preferred_element_type=f32)


def _dot_tn(a, b):
    return lax.dot_general(a, b, (((0,), (0,)), ((), ())), preferred_element_type=f32)


def _layer_norm(x, g, b):
    mu = jnp.mean(x, axis=-1, keepdims=True)
    xc = x - mu
    var = jnp.mean(xc * xc, axis=-1, keepdims=True)
    return xc * lax.rsqrt(var + LN_EPS) * g + b


def _const_spec(shape):
    nd = len(shape)
    return pl.BlockSpec(shape, lambda *_: (0,) * nd, pipeline_mode=pl.Buffered(1))


def _params(*sem):
    return pltpu.CompilerParams(dimension_semantics=sem, vmem_limit_bytes=VMEM_LIMIT)


def _inproj_kernel(h_ref, w_ref, cqg_ref, wuq_ref, wuqi_ref, kig_ref, kib_ref, loglb_ref, log1mlb_ref, onemlb_ref,
                   qa_ref, ka_ref, va_ref, qi_ref, kl_ref, kr_ref, wi_ref, qb_ref, logf_ref, kb_ref, ib_ref, gb_ref):
    hb = h_ref[...].astype(bf16)

    pa = _dot(hb, w_ref[:, _SEG_A[0]:_SEG_A[1]])
    cq = pa[:, :Q_RANK]
    c = cq * lax.rsqrt(jnp.mean(cq * cq, axis=-1, keepdims=True) + LN_EPS) * cqg_ref[...]
    cb = c.astype(bf16)
    qa_ref[...] = (_dot(cb, wuq_ref[...]) * (A_HEAD_DIM ** -0.5)).astype(bf16)
    qi_ref[...] = _dot(cb, wuqi_ref[...]).astype(bf16)
    ka_ref[...] = pa[:, Q_RANK:Q_RANK + A_WIDTH].astype(bf16)
    va_ref[...] = pa[:, Q_RANK + A_WIDTH:].astype(bf16)

    pk = _dot(hb, w_ref[:, _SEG_K[0]:_SEG_K[1]])
    kd = pk[:, :LANES]
    left = lax.broadcasted_iota(jnp.int32, (1, LANES), 1) < IDX_DIM
    mu = jnp.sum(jnp.where(left, kd, 0.0), axis=-1, keepdims=True) * (1.0 / IDX_DIM)
    xc = kd - mu
    var = jnp.sum(jnp.where(left, xc * xc, 0.0), axis=-1, keepdims=True) * (1.0 / IDX_DIM)
    kn = xc * lax.rsqrt(var + LN_EPS) * kig_ref[...] + kib_ref[...]
    kl_ref[...] = jnp.where(left, kn, 0.0).astype(bf16)
    kr_ref[...] = jnp.where(left, 0.0, kn).astype(bf16)
    wi_ref[...] = pk[:, LANES:LANES + IDX_HEADS] * (IDX_HEADS ** -0.5 * IDX_DIM ** -0.5)

    pb = _dot(hb, w_ref[:, _SEG_B[0]:_SEG_B[1]])
    qb_ref[...] = pb[:, :B_WIDTH]
    z = pb[:, B_WIDTH:2 * B_WIDTH]
    ib_ref[...] = pb[:, 2 * B_WIDTH:3 * B_WIDTH]
    gb_ref[...] = pb[:, 3 * B_WIDTH:]
    e = jnp.exp(-jnp.abs(z))
    log_sig = jnp.minimum(z, 0.0) - jnp.log1p(e)
    a = loglb_ref[...]
    b = log1mlb_ref[...] + log_sig
    logf_ref[...] = jnp.maximum(a, b) + jnp.log1p(jnp.exp(-jnp.abs(a - b)))
    kb_ref[...] = onemlb_ref[...] * (jnp.where(z >= 0.0, e, 1.0) / (1.0 + e))


def _inproj(h, w_cat, cq_gain, w_uq, w_uq_idx, kig2, kib2, loglb, log1mlb, onemlb, *, tm):
    B, Lp, D = h.shape
    row = lambda c: pl.BlockSpec((None, tm, c), lambda b, i: (b, i, 0))
    outs = [(A_WIDTH, bf16), (A_WIDTH, bf16), (A_WIDTH, bf16), (IDX_HEADS * IDX_DIM, bf16), (LANES, bf16), (LANES, bf16),
            (IDX_HEADS, f32), (B_WIDTH, f32), (B_WIDTH, f32), (B_WIDTH, f32), (B_WIDTH, f32), (B_WIDTH, f32)]
    consts = [w_cat, cq_gain, w_uq, w_uq_idx, kig2, kib2, loglb, log1mlb, onemlb]
    return pl.pallas_call(
        _inproj_kernel,
        grid=(B, Lp // tm),
        in_specs=[row(D)] + [_const_spec(c.shape) for c in consts],
        out_specs=[row(c) for c, _ in outs],
        out_shape=[jax.ShapeDtypeStruct((B, Lp, c), dt) for c, dt in outs],
        compiler_params=_params("parallel", "parallel"),
        name="inproj",
    )(h, *consts)


def _dsa_kernel(qa_ref, qi_ref, wi_ref, kl_ref, kr_ref, ka_ref, va_ref, o_ref,
                keys_scr, wbc_scr, qm_scr, m_scr, l_scr, acc_scr, *, topk, tq):
    i = pl.program_id(1)
    nkt = i + 1
    left = lax.broadcasted_iota(jnp.int32, (1, LANES), 1) < A_HEAD_DIM
    row_pos = i * tq + lax.broadcasted_iota(jnp.int32, (tq, 1), 0)
    col_iota = lax.broadcasted_iota(jnp.int32, (1, tq), 1)

    for h in range(IDX_HEADS):
        wbc_scr[h] = jnp.broadcast_to(wi_ref[:, h:h + 1], (tq, tq))
    for p in range(A_HEADS // 2):
        qp = qa_ref[:, p * LANES:(p + 1) * LANES]
        qm_scr[2 * p] = jnp.where(left, qp, jnp.zeros_like(qp))
        qm_scr[2 * p + 1] = jnp.where(left, jnp.zeros_like(qp), qp)

    def score_body(kt, carry):
        k0 = pl.multiple_of(kt * tq, tq)
        kl = kl_ref[pl.ds(k0, tq), :]
        kr = kr_ref[pl.ds(k0, tq), :]
        acc = jnp.zeros((tq, tq), f32)
        for p in range(IDX_HEADS // 2):
            qp = qi_ref[:, p * LANES:(p + 1) * LANES]
            acc = acc + jnp.maximum(_dot_nt(qp, kl), 0.0) * wbc_scr[2 * p]
            acc = acc + jnp.maximum(_dot_nt(qp, kr), 0.0) * wbc_scr[2 * p + 1]
        bits = pltpu.bitcast(acc, jnp.int32)
        key = bits ^ ((bits >> 31) & jnp.int32(0x7FFFFFFF))
        causal = (k0 + col_iota) <= row_pos
        keys_scr[kt] = jnp.where(causal, key, jnp.int32(INT_MIN))
        return carry

    lax.fori_loop(0, nkt, score_body, 0)

    def bit_body(it, c):
        trial = c + jnp.left_shift(jnp.int32(1), 31 - it)

        def cnt_body(kt, a):
            ge = (keys_scr[kt] >= trial).astype(jnp.int32)
            for j in range(tq // LANES):
                a = a + ge[:, j * LANES:(j + 1) * LANES]
            return a

        a = lax.fori_loop(0, nkt, cnt_body, jnp.zeros((tq, LANES), jnp.int32))
        cnt = jnp.sum(a, axis=1, keepdims=True)
        return jnp.where(cnt >= topk, trial, c)

    c = lax.fori_loop(0, 32, bit_body, jnp.full((tq, 1), INT_MIN, jnp.int32))
    thr = jnp.maximum(c, jnp.int32(INT_MIN + 1))

    m_scr[...] = jnp.full(m_scr.shape, NEG_BIG, f32)
    l_scr[...] = jnp.zeros(l_scr.shape, f32)
    acc_scr[...] = jnp.zeros(acc_scr.shape, f32)

    def attn_body(kt, carry):
        k0 = pl.multiple_of(kt * tq, tq)
        sel = keys_scr[kt] >= thr
        dist = (row_pos - (k0 + col_iota)).astype(f32)
        for p in range(A_HEADS // 2):
            kp = ka_ref[pl.ds(k0, tq), p * LANES:(p + 1) * LANES]
            vp = va_ref[pl.ds(k0, tq), p * LANES:(p + 1) * LANES]
            pvs, alphas = [], []
            for e in range(2):
                hh = 2 * p + e
                s = _dot_nt(qm_scr[hh], kp)
                s = jnp.where(sel, s - (2.0 ** -(hh + 1)) * dist, NEG_BIG)
                m_old = m_scr[hh]
                m_new = jnp.maximum(m_old, jnp.max(s, axis=1, keepdims=True))
                alpha = jnp.exp(m_old - m_new)
                pm = jnp.exp(s - m_new)
                l_scr[hh] = alpha * l_scr[hh] + jnp.sum(pm, axis=1, keepdims=True)
                m_scr[hh] = m_new
                pvs.append(_dot(pm.astype(bf16), vp))
                alphas.append(alpha)
            a_pair = jnp.where(left, alphas[0], alphas[1])
            sl = slice(p * LANES, (p + 1) * LANES)
            acc_scr[:, sl] = a_pair * acc_scr[:, sl] + jnp.where(left, pvs[0], pvs[1])
        return carry

    lax.fori_loop(0, nkt, attn_body, 0)

    for p in range(A_HEADS // 2):
        sl = slice(p * LANES, (p + 1) * LANES)
        l_pair = jnp.where(left, l_scr[2 * p], l_scr[2 * p + 1])
        o_ref[:, sl] = (acc_scr[:, sl] / l_pair).astype(o_ref.dtype)


def _dsa(qa, qi, wi, kl, kr, ka, va, *, topk, tq):
    B, Lp, _ = qa.shape
    nq = Lp // tq
    qrow = lambda c: pl.BlockSpec((None, tq, c), lambda b, i: (b, i, 0))
    seq = lambda c: pl.BlockSpec((None, Lp, c), lambda b, i: (b, 0, 0), pipeline_mode=pl.Buffered(1))
    return pl.pallas_call(
        functools.partial(_dsa_kernel, topk=topk, tq=tq),
        grid=(B, nq),
        in_specs=[qrow(A_WIDTH), qrow(IDX_HEADS * IDX_DIM), qrow(IDX_HEADS),
                  seq(LANES), seq(LANES), seq(A_WIDTH), seq(A_WIDTH)],
        out_specs=qrow(A_WIDTH),
        out_shape=jax.ShapeDtypeStruct((B, Lp, A_WIDTH), bf16),
        scratch_shapes=[
            pltpu.VMEM((nq, tq, tq), jnp.int32),
            pltpu.VMEM((IDX_HEADS, tq, tq), f32),
            pltpu.VMEM((A_HEADS, tq, LANES), bf16),
            pltpu.VMEM((A_HEADS, tq, 1), f32),
            pltpu.VMEM((A_HEADS, tq, 1), f32),
            pltpu.VMEM((tq, A_WIDTH), f32),
        ],
        compiler_params=_params("parallel", "arbitrary"),
        name="dsa",
    )(qa, qi, wi, kl, kr, ka, va)


def _hgrn2_kernel(qb_ref, logf_ref, kb_ref, ib_ref, gb_ref, gain_ref, tril_ref, bones_ref, bmask_ref, o_ref,
                  st_scr, intra_scr):
    C = HG_CHUNK

    @pl.when(pl.program_id(1) == 0)
    def _():
        st_scr[...] = jnp.zeros(st_scr.shape, f32)

    lf = logf_ref[...]
    hi = lf.astype(bf16)
    r1 = lf - hi.astype(f32)
    mid = r1.astype(bf16)
    lo = (r1 - mid.astype(f32)).astype(bf16)
    tril = tril_ref[...]
    b = _dot(tril, hi) + _dot(tril, mid) + _dot(tril, lo)

    q = qb_ref[...]
    k = kb_ref[...]
    v = ib_ref[...]
    bones = bones_ref[...]

    inter = _dot_nt((q * jnp.exp(b)).astype(bf16), st_scr[...].astype(bf16))
    b_last = b[C - 1:C, :]
    k_dec = (k * jnp.exp(b_last - b)).astype(bf16)
    upd = _dot_tn(v.astype(bf16), k_dec)
    st_scr[...] = st_scr[...] * jnp.exp(b_last) + upd * bmask_ref[...]

    sidx = lax.broadcasted_iota(jnp.int32, (C, 1), 0)
    for t in range(C):
        ns = SUBLANES * (t // SUBLANES + 1)
        diff = jnp.where(sidx[:ns] <= t, b[t:t + 1, :] - b[:ns], -jnp.inf)
        a = (q[t:t + 1, :] * jnp.exp(diff)) * k[:ns]
        w = _dot(a.astype(bf16), bones)
        intra_scr[t:t + 1, :] = jnp.sum(w * v[:ns], axis=0, keepdims=True)

    o = inter + intra_scr[...]
    o2 = o * o
    o2_hi = o2.astype(bf16)
    o2_lo = (o2 - o2_hi.astype(f32)).astype(bf16)
    ss = _dot(o2_hi, bones) + _dot(o2_lo, bones)
    on = o * lax.rsqrt(ss * (1.0 / B_KEY_DIM) + LN_EPS) * gain_ref[...]
    g = gb_ref[...]
    o_ref[...] = (on * (g / (1.0 + jnp.exp(-g)))).astype(o_ref.dtype)


def _hgrn2(qb, logf, kb, ib, gb, gain):
    B, Lp, W = qb.shape
    C = HG_CHUNK
    head = np.arange(W) // B_KEY_DIM
    blk = (head[:, None] == head[None, :])
    tril = jnp.asarray(np.tril(np.ones((C, C), np.float32)), bf16)
    bones = jnp.asarray(blk.astype(np.float32), bf16)
    bmask = jnp.asarray(blk.astype(np.float32), f32)
    row = pl.BlockSpec((None, C, W), lambda b, c: (b, c, 0))
    consts = [gain, tril, bones, bmask]
    return pl.pallas_call(
        _hgrn2_kernel,
        grid=(B, Lp // C),
        in_specs=[row] * 5 + [_const_spec(c.shape) for c in consts],
        out_specs=row,
        out_shape=jax.ShapeDtypeStruct((B, Lp, W), bf16),
        scratch_shapes=[pltpu.VMEM((W, W), f32), pltpu.VMEM((C, W), f32)],
        compiler_params=_params("parallel", "arbitrary"),
        name="hgrn2",
    )(qb, logf, kb, ib, gb, *consts)


def _ffn_tail(h, mix, g1_ref, b1_ref, wg_ref, wu_ref, wd_ref, g2_ref, b2_ref, out_ref):
    h1 = _layer_norm(ALPHA * h + mix, g1_ref[...], b1_ref[...])
    hb = h1.astype(bf16)
    g = _dot(hb, wg_ref[...])
    u = _dot(hb, wu_ref[...])
    act = (g / (1.0 + jnp.exp(-g))) * u
    f = _dot(act.astype(bf16), wd_ref[...])
    out_ref[...] = _layer_norm(ALPHA * h1 + f, g2_ref[...], b2_ref[...])


def _post_even_kernel(h_ref, oa_ref, ob_ref, wout_ref, *rest):
    mix = _dot(oa_ref[...], wout_ref[:A_WIDTH, :]) + _dot(ob_ref[...], wout_ref[A_WIDTH:, :])
    _ffn_tail(h_ref[...], mix, *rest)


def _post_odd_kernel(h_ref, prev_ref, wpool_ref, pscale_ref, *rest, tm):
    i = pl.program_id(1)
    h = h_ref[...]
    prev = jnp.where(i == 0, 0.0, prev_ref[...])
    x = jnp.concatenate([prev, h], axis=0)
    pos = i * tm + lax.broadcasted_iota(jnp.int32, (tm, 1), 0)
    parts = []
    for g, win in enumerate(POOL_WINDOWS):
        sl = slice(g * POOL_GROUP_DIM, (g + 1) * POOL_GROUP_DIM)
        s = x[:, sl]
        span = 1
        while span < win:
            s = s + pltpu.roll(s, span, axis=0)
            span *= 2
        cnt = jnp.minimum(pos + 1, win).astype(f32)
        d = s[POOL_HALO:, :] / cnt - h[:, sl]
        parts.append(_dot(d.astype(bf16), wpool_ref[g]))
    mix = jnp.concatenate(parts, axis=-1) * pscale_ref[...]
    _ffn_tail(h, mix, *rest)


def _post(h, mixer_inputs, mixer_specs, kernel, tail_consts, *, tm):
    B, Lp, D = h.shape
    row = pl.BlockSpec((None, tm, D), lambda b, i: (b, i, 0))
    return pl.pallas_call(
        kernel,
        grid=(B, Lp // tm),
        in_specs=[row] + mixer_specs + [_const_spec(c.shape) for c in tail_consts],
        out_specs=row,
        out_shape=jax.ShapeDtypeStruct((B, Lp, D), f32),
        compiler_params=_params("parallel", "parallel"),
        name=getattr(kernel, "__name__", "post_odd"),
    )(h, *mixer_inputs, *tail_consts)


def _post_even(h, oa, ob, w_out, tail_consts, *, tm):
    row = lambda c: pl.BlockSpec((None, tm, c), lambda b, i: (b, i, 0))
    return _post(h, [oa, ob, w_out], [row(A_WIDTH), row(B_WIDTH), _const_spec(w_out.shape)],
                 _post_even_kernel, tail_consts, tm=tm)


def _post_odd(h, w_pool, pool_scale, tail_consts, *, tm):
    D = h.shape[-1]
    halo_per_tile = tm // POOL_HALO
    prev = pl.BlockSpec((None, POOL_HALO, D), lambda b, i: (b, jnp.maximum(i * halo_per_tile - 1, 0), 0))
    kern = functools.partial(_post_odd_kernel, tm=tm)
    kern.__name__ = "post_odd"
    return _post(h, [h, w_pool, pool_scale], [prev, _const_spec(w_pool.shape), _const_spec(pool_scale.shape)],
                 kern, tail_consts, tm=tm)


def _pack_w_in(w):
    o = np.cumsum([0, Q_RANK, A_WIDTH, A_WIDTH, IDX_DIM, IDX_HEADS, B_WIDTH, B_WIDTH, B_WIDTH, B_WIDTH])
    seg = lambda j: w[:, o[j]:o[j + 1]]
    pad = jnp.zeros((w.shape[0], LANES - IDX_HEADS), w.dtype)
    cat = jnp.concatenate([seg(0), seg(1), seg(2), seg(3), seg(3), seg(4), pad, seg(5), seg(6), seg(7), seg(8)], axis=1)
    return cat.astype(bf16)


def kernel(x, meta_tokens, w_in, cq_gain, w_uq, w_uq_idx, kidx_gain, kidx_bias, lb_raw, onorm_gain, w_out,
           w_pool, pool_scale, ln_mix_g, ln_mix_b, w_gate, w_up, w_down, ln_ffn_g, ln_ffn_b):
    B, S, D = x.shape
    L = N_META + S
    Lp = -(-L // SEQ_TILE) * SEQ_TILE
    topk = min(TOPK_MAX, S // 4)
    tm = SEQ_TILE

    meta = jnp.broadcast_to(meta_tokens.astype(x.dtype)[None], (B, N_META, D))
    h = jnp.concatenate([meta, x, jnp.zeros((B, Lp - L, D), x.dtype)], axis=1)

    lower = jnp.cumsum(jax.nn.softmax(lb_raw.astype(f32), axis=0), axis=0)
    lower = lower - lower[:1]

    row = lambda v: v.reshape(1, -1).astype(f32)
    for layer in range(DEPTH):
        j = layer // 2
        tail = [row(ln_mix_g[layer]), row(ln_mix_b[layer]), w_gate[layer].astype(bf16), w_up[layer].astype(bf16),
                w_down[layer].astype(bf16), row(ln_ffn_g[layer]), row(ln_ffn_b[layer])]
        if layer % 2 == 0:
            lb = lower[j]
            qa, ka, va, qi, kl, kr, wi, qb, logf, kb, ib, gb = _inproj(
                h, _pack_w_in(w_in[j]), row(cq_gain[j]), w_uq[j].astype(bf16), w_uq_idx[j].astype(bf16),
                row(jnp.tile(kidx_gain[j], 2)), row(jnp.tile(kidx_bias[j], 2)),
                row(jnp.log(lb)), row(jnp.log1p(-lb)), row(1.0 - lb), tm=tm)
            oa = _dsa(qa, qi, wi, kl, kr, ka, va, topk=topk, tq=SEQ_TILE)
            ob = _hgrn2(qb, logf, kb, ib, gb, row(onorm_gain[j]))
            h = _post_even(h, oa, ob, w_out[j].astype(bf16), tail, tm=tm)
        else:
            h = _post_odd(h, w_pool[j].astype(bf16), row(pool_scale[j]), tail, tm=tm)
    return h[:, N_META:L]
```

```python
import functools

import jax
import jax.numpy as jnp
import numpy as np
from jax import lax
from jax.experimental import pallas as pl
from jax.experimental.pallas import tpu as pltpu

D_MODEL = 1024
DEPTH = 4
N_META = 16
A_HEADS = 8
A_HEAD_DIM = 64
A_WIDTH = A_HEADS * A_HEAD_DIM
Q_RANK = 256
IDX_HEADS = 16
IDX_DIM = 64
TOPK_MAX = 256
B_HEADS = 8
B_KEY_DIM = 64
B_WIDTH = B_HEADS * B_KEY_DIM
POOL_WINDOWS = (2, 4, 8, 16)
POOL_GROUP_DIM = D_MODEL // len(POOL_WINDOWS)
D_FF = -(-8 * D_MODEL // (3 * 256)) * 256
ALPHA = (2 * DEPTH) ** 0.25
LN_EPS = 1e-5

LANES = 128
SUBLANES = 8
MXU_DIM = 256
VMEM_LIMIT = 56 * 1024 * 1024

SEQ_TILE = 256
HG_CHUNK = 64
POOL_HALO = 16

_SEG_A = (0, Q_RANK + A_WIDTH)
_SEG_K = (_SEG_A[1], _SEG_A[1] + LANES)
_SEG_B = (_SEG_K[1], _SEG_K[1] + 4 * B_WIDTH)
IN_COLS_PADDED = _SEG_B[1]

INT_MIN = -2 ** 31
NEG_BIG = -1e30

bf16 = jnp.bfloat16
f32 = jnp.float32


def _dot(a, b):
    return jnp.dot(a, b, preferred_element_type=f32)


def _dot_nt(a, b):
    return lax.dot_general(a, b, (((1,), (1,)), ((), ())), preferred_element_type=f32)


def _dot_tn(a, b):
    return lax.dot_general(a, b, (((0,), (0,)), ((), ())), preferred_element_type=f32)


def _layer_norm(x, g, b):
    mu = jnp.mean(x, axis=-1, keepdims=True)
    xc = x - mu
    var = jnp.mean(xc * xc, axis=-1, keepdims=True)
    return xc * lax.rsqrt(var + LN_EPS) * g + b


def _const_spec(shape):
    nd = len(shape)
    return pl.BlockSpec(shape, lambda *_: (0,) * nd, pipeline_mode=pl.Buffered(1))


def _params(*sem):
    return pltpu.CompilerParams(dimension_semantics=sem, vmem_limit_bytes=VMEM_LIMIT)


def _inproj_kernel(h_ref, w_ref, wvt_ref, wwt_ref, cqg_ref, wuq_ref, wuqi_ref, kig_ref, kib_ref,
                   loglb_ref, log1mlb_ref, onemlb_ref,
                   qa_ref, ka_ref, vt_ref, qi_ref, klr_ref, wit_ref, qb_ref, logf_ref, kb_ref, ib_ref, gb_ref):
    tm = h_ref.shape[0]
    hb = h_ref[...].astype(bf16)

    pa = _dot(hb, w_ref[:, _SEG_A[0]:_SEG_A[1]])
    cq = pa[:, :Q_RANK]
    c = cq * lax.rsqrt(jnp.mean(cq * cq, axis=-1, keepdims=True) + LN_EPS) * cqg_ref[...]
    cb = c.astype(bf16)
    qa_ref[...] = (_dot(cb, wuq_ref[...]) * (A_HEAD_DIM ** -0.5)).astype(bf16)
    qi_ref[...] = _dot(cb, wuqi_ref[...]).astype(bf16)
    ka_ref[...] = pa[:, Q_RANK:].astype(bf16)
    vt_ref[...] = _dot_nt(wvt_ref[...], hb).astype(bf16)
    wit_ref[...] = _dot_nt(wwt_ref[...], hb) * (IDX_HEADS ** -0.5 * IDX_DIM ** -0.5)

    kd = _dot(hb, w_ref[:, _SEG_K[0]:_SEG_K[1]])
    left = lax.broadcasted_iota(jnp.int32, (1, LANES), 1) < IDX_DIM
    mu = jnp.sum(jnp.where(left, kd, 0.0), axis=-1, keepdims=True) * (1.0 / IDX_DIM)
    xc = kd - mu
    var = jnp.sum(jnp.where(left, xc * xc, 0.0), axis=-1, keepdims=True) * (1.0 / IDX_DIM)
    kn = xc * lax.rsqrt(var + LN_EPS) * kig_ref[...] + kib_ref[...]
    klr_ref[:tm, :] = jnp.where(left, kn, 0.0).astype(bf16)
    klr_ref[tm:, :] = jnp.where(left, 0.0, kn).astype(bf16)

    pb = _dot(hb, w_ref[:, _SEG_B[0]:_SEG_B[1]])
    z = pb[:, B_WIDTH:2 * B_WIDTH]
    e = jnp.exp(-jnp.abs(z))
    log_sig = jnp.minimum(z, 0.0) - jnp.log1p(e)
    a = loglb_ref[...]
    b = log1mlb_ref[...] + log_sig
    logf = jnp.maximum(a, b) + jnp.log1p(jnp.exp(-jnp.abs(a - b)))
    kb = onemlb_ref[...] * (jnp.where(z >= 0.0, e, 1.0) / (1.0 + e))
    for p in range(B_WIDTH // LANES):
        sl = slice(p * LANES, (p + 1) * LANES)
        qb_ref[p] = pb[:, sl]
        logf_ref[p] = logf[:, sl]
        kb_ref[p] = kb[:, sl]
        ib_ref[p] = pb[:, 2 * B_WIDTH + p * LANES:2 * B_WIDTH + (p + 1) * LANES]
        gb_ref[p] = pb[:, 3 * B_WIDTH + p * LANES:3 * B_WIDTH + (p + 1) * LANES]


def _inproj(h, w_cat, w_vt, w_wt, cq_gain, w_uq, w_uq_idx, kig2, kib2, loglb, log1mlb, onemlb, *, tm):
    B, Lp, D = h.shape
    nt = Lp // tm
    row = lambda c: pl.BlockSpec((None, tm, c), lambda b, i: (b, i, 0))
    tile = lambda r, c: pl.BlockSpec((None, None, r, c), lambda b, i: (b, i, 0, 0))
    tok = lambda c, dt: (row(c), jax.ShapeDtypeStruct((B, Lp, c), dt))
    til = lambda r, c, dt: (tile(r, c), jax.ShapeDtypeStruct((B, nt, r, c), dt))
    npair = B_WIDTH // LANES
    pair = (pl.BlockSpec((None, npair, tm, LANES), lambda b, i: (b, 0, i, 0)),
            jax.ShapeDtypeStruct((B, npair, Lp, LANES), f32))
    outs = [tok(A_WIDTH, bf16), tok(A_WIDTH, bf16), til(A_WIDTH, tm, bf16), tok(IDX_HEADS * IDX_DIM, bf16),
            til(2 * tm, LANES, bf16), til(IDX_HEADS, tm, f32),
            pair, pair, pair, pair, pair]
    consts = [w_cat, w_vt, w_wt, cq_gain, w_uq, w_uq_idx, kig2, kib2, loglb, log1mlb, onemlb]
    return pl.pallas_call(
        _inproj_kernel,
        grid=(B, nt),
        in_specs=[row(D)] + [_const_spec(c.shape) for c in consts],
        out_specs=[s for s, _ in outs],
        out_shape=[o for _, o in outs],
        compiler_params=_params("parallel", "parallel"),
        name="inproj",
    )(h, *consts)


def _tree(op, xs):
    xs = list(xs)
    while len(xs) > 1:
        xs = [op(xs[j], xs[j + 1]) for j in range(0, len(xs) - 1, 2)] + ([xs[-1]] if len(xs) % 2 else [])
    return xs[0]


def _dsa_kernel(qa_ref, qi_ref, wit_ref, klr_ref, ka_ref, vt_ref, pos_ref, o_ref,
                keys_scr, bias_scr, sacc_scr, qaug_scr, pm_scr, acc_scr, *, topk, tq):
    i = pl.program_id(1)
    nkt = i + 1
    lane = lax.broadcasted_iota(jnp.int32, (1, LANES), 1)
    left = lane < A_HEAD_DIM
    q_pos = i * tq + lax.broadcasted_iota(jnp.int32, (1, tq), 1)
    k_iota = lax.broadcasted_iota(jnp.int32, (tq, 1), 0)

    for h in range(A_HEADS):
        slope = 2.0 ** -(h + 1)
        qp = qa_ref[:, (h // 2) * LANES:(h // 2 + 1) * LANES]
        keep = left if h % 2 == 0 else jnp.logical_not(left)
        coef = jnp.where(lane == 0, slope * tq, jnp.where(lane == 1, slope, 0.0)).astype(bf16)
        qaug_scr[h, :, :LANES] = jnp.where(keep, qp, jnp.zeros_like(qp))
        qaug_scr[h, :, LANES:] = jnp.broadcast_to(coef, (tq, LANES))

    def score_body(kt, carry):
        klr = klr_ref[kt]
        for p in range(IDX_HEADS // 2):
            s = _dot_nt(klr, qi_ref[:, p * LANES:(p + 1) * LANES])
            t = (jnp.maximum(s[:tq], 0.0) * wit_ref[2 * p:2 * p + 1, :]
                 + jnp.maximum(s[tq:], 0.0) * wit_ref[2 * p + 1:2 * p + 2, :])
            if p == 0:
                sacc_scr[...] = t
            else:
                sacc_scr[...] += t
        bits = pltpu.bitcast(sacc_scr[...], jnp.int32)
        key = bits ^ ((bits >> 31) & jnp.int32(0x7FFFFFFF))
        causal = (kt * tq + k_iota) <= q_pos
        keys_scr[kt] = jnp.where(causal, key, jnp.int32(INT_MIN))
        return carry

    lax.fori_loop(0, nkt, score_body, 0)

    def bit_body(it, c):
        trial = c + jnp.left_shift(jnp.int32(1), 31 - it)
        trial_b = jnp.broadcast_to(trial, (SUBLANES, tq))

        def cnt_body(kt, a):
            return a + _tree(jnp.add, ((keys_scr[kt, r * SUBLANES:(r + 1) * SUBLANES, :] >= trial_b).astype(jnp.int32)
                                       for r in range(tq // SUBLANES)))

        a = lax.fori_loop(0, nkt, cnt_body, jnp.zeros((SUBLANES, tq), jnp.int32))
        cnt = jnp.sum(a, axis=0, keepdims=True)
        return jnp.where(cnt >= topk, trial, c)

    c = lax.fori_loop(0, 32, bit_body, jnp.full((1, tq), INT_MIN, jnp.int32))
    thr = jnp.maximum(c, jnp.int32(INT_MIN + 1))

    def bias_body(kt, carry):
        bias_scr[kt] = jnp.where(keys_scr[kt] >= thr, 0.0, NEG_BIG)
        return carry

    lax.fori_loop(0, nkt, bias_body, 0)

    groups = tq // SUBLANES
    row_groups = lambda x: (x[r * SUBLANES:(r + 1) * SUBLANES] for r in range(groups))

    def logits(kt, h):
        k0 = pl.multiple_of(kt * tq, tq)
        kaug = jnp.concatenate([ka_ref[pl.ds(k0, tq), (h // 2) * LANES:(h // 2 + 1) * LANES],
                                pos_ref[pl.ds(k0, tq), :]], axis=1)
        return _dot_nt(kaug, qaug_scr[h]) + bias_scr[kt]

    def max_body(kt, m8):
        return tuple(jnp.maximum(m8[h], _tree(jnp.maximum, row_groups(logits(kt, h)))) for h in range(A_HEADS))

    m8 = lax.fori_loop(0, nkt, max_body, (jnp.full((SUBLANES, tq), NEG_BIG, f32),) * A_HEADS)
    m = [jnp.max(m8[h], axis=0, keepdims=True) for h in range(A_HEADS)]

    acc_scr[...] = jnp.zeros(acc_scr.shape, f32)

    def acc_body(kt, l8):
        out = []
        for h in range(A_HEADS):
            pm = jnp.exp(logits(kt, h) - m[h])
            out.append(l8[h] + _tree(jnp.add, row_groups(pm)))
            pm_scr[h] = pm.astype(bf16)
        for h in range(A_HEADS):
            hs = slice(h * A_HEAD_DIM, (h + 1) * A_HEAD_DIM)
            acc_scr[hs, :] += _dot(vt_ref[kt, hs, :], pm_scr[h])
        return tuple(out)

    l8 = lax.fori_loop(0, nkt, acc_body, (jnp.zeros((SUBLANES, tq), f32),) * A_HEADS)
    for h in range(A_HEADS):
        hs = slice(h * A_HEAD_DIM, (h + 1) * A_HEAD_DIM)
        acc_scr[hs, :] = acc_scr[hs, :] / jnp.sum(l8[h], axis=0, keepdims=True)
    o_ref[...] = acc_scr[...].T.astype(o_ref.dtype)


def _dsa(qa, qi, wit, klr, ka, vt, *, topk, tq):
    B, Lp, _ = qa.shape
    nq = Lp // tq
    kpos = np.arange(Lp)
    posfeat = np.zeros((Lp, LANES), np.float32)
    posfeat[:, 0] = kpos // tq
    posfeat[:, 1] = kpos % tq
    posfeat = jnp.asarray(posfeat, bf16)
    qrow = lambda c: pl.BlockSpec((None, tq, c), lambda b, i: (b, i, 0))
    seq = lambda *s: pl.BlockSpec((None,) + s, lambda b, i: (b,) + (0,) * len(s), pipeline_mode=pl.Buffered(1))
    return pl.pallas_call(
        functools.partial(_dsa_kernel, topk=topk, tq=tq),
        grid=(B, nq),
        in_specs=[qrow(A_WIDTH), qrow(IDX_HEADS * IDX_DIM),
                  pl.BlockSpec((None, None, IDX_HEADS, tq), lambda b, i: (b, i, 0, 0)),
                  seq(nq, 2 * tq, LANES), seq(Lp, A_WIDTH), seq(nq, A_WIDTH, tq), _const_spec(posfeat.shape)],
        out_specs=qrow(A_WIDTH),
        out_shape=jax.ShapeDtypeStruct((B, Lp, A_WIDTH), bf16),
        scratch_shapes=[
            pltpu.VMEM((nq, tq, tq), jnp.int32),
            pltpu.VMEM((nq, tq, tq), f32),
            pltpu.VMEM((tq, tq), f32),
            pltpu.VMEM((A_HEADS, tq, 2 * LANES), bf16),
            pltpu.VMEM((A_HEADS, tq, tq), bf16),
            pltpu.VMEM((A_WIDTH, tq), f32),
        ],
        compiler_params=_params("parallel", "arbitrary"),
        name="dsa",
    )(qa, qi, wit, klr, ka, vt, posfeat)


def _hgrn2_offsets(C):
    ns = [SUBLANES * (t // SUBLANES + 1) for t in range(C)]
    return ns, np.concatenate([[0], np.cumsum(ns)]).tolist()


def _hgrn2_kernel(qb_ref, logf_ref, kb_ref, ib_ref, gb_ref, gain_ref, tril_ref, bones_ref, o_ref,
                  st_scr, a_scr, w_scr, intra_scr):
    C = HG_CHUNK
    P = qb_ref.shape[0]
    pairs = range(P)
    ns, off = _hgrn2_offsets(C)

    @pl.when(pl.program_id(1) == 0)
    def _():
        st_scr[...] = jnp.zeros(st_scr.shape, f32)

    tril = tril_ref[...]
    bones = bones_ref[...]

    def cumsum(lf):
        hi = lf.astype(bf16)
        r1 = lf - hi.astype(f32)
        mid = r1.astype(bf16)
        lo = (r1 - mid.astype(f32)).astype(bf16)
        return _dot(tril, hi) + _dot(tril, mid) + _dot(tril, lo)

    b = [cumsum(logf_ref[p]) for p in pairs]
    q = [qb_ref[p] for p in pairs]
    k = [kb_ref[p] for p in pairs]
    v = [ib_ref[p] for p in pairs]

    inter = [_dot_nt((q[p] * jnp.exp(b[p])).astype(bf16), st_scr[p].astype(bf16)) for p in pairs]
    for p in pairs:
        b_last = b[p][C - 1:C, :]
        k_dec = (k[p] * jnp.exp(b_last - b[p])).astype(bf16)
        upd = _dot_tn(v[p].astype(bf16), k_dec)
        st_scr[p] = st_scr[p] * jnp.exp(b_last) + upd * bones.astype(f32)

    sidx = lax.broadcasted_iota(jnp.int32, (SUBLANES, 1), 0)
    for p in pairs:
        for t in range(C):
            n = ns[t]
            diff = b[p][t:t + 1, :] - b[p][:n]
            last = jnp.where(sidx <= t % SUBLANES, diff[n - SUBLANES:], -jnp.inf)
            diff = last if n == SUBLANES else jnp.concatenate([diff[:n - SUBLANES], last], axis=0)
            a_scr[p, off[t]:off[t] + n, :] = (q[p][t:t + 1, :] * jnp.exp(diff)) * k[p][:n]
    for p in pairs:
        w_scr[p] = _dot(a_scr[p].astype(bf16), bones)
    for p in pairs:
        for t in range(C):
            n = ns[t]
            intra_scr[p, t:t + 1, :] = jnp.sum(w_scr[p, off[t]:off[t] + n, :] * v[p][:n], axis=0, keepdims=True)

    o = [inter[p] + intra_scr[p] for p in pairs]
    ss = []
    for p in pairs:
        o2 = o[p] * o[p]
        o2_hi = o2.astype(bf16)
        o2_lo = (o2 - o2_hi.astype(f32)).astype(bf16)
        ss.append(_dot(o2_hi, bones) + _dot(o2_lo, bones))
    for p in pairs:
        on = o[p] * lax.rsqrt(ss[p] * (1.0 / B_KEY_DIM) + LN_EPS) * gain_ref[p]
        g = gb_ref[p]
        o_ref[p] = (on * (g / (1.0 + jnp.exp(-g)))).astype(o_ref.dtype)


def _hgrn2(qb, logf, kb, ib, gb, gain):
    B, P, Lp, W = qb.shape
    C = HG_CHUNK
    head = np.arange(W) // B_KEY_DIM
    tril = jnp.asarray(np.tril(np.ones((C, C), np.float32)), bf16)
    bones = jnp.asarray((head[:, None] == head[None, :]).astype(np.float32), bf16)
    rows = _hgrn2_offsets(C)[1][-1]
    row = pl.BlockSpec((None, P, C, W), lambda b, c: (b, 0, c, 0))
    return pl.pallas_call(
        _hgrn2_kernel,
        grid=(B, Lp // C),
        in_specs=[row] * 5 + [_const_spec(gain.shape), _const_spec(tril.shape), _const_spec(bones.shape)],
        out_specs=row,
        out_shape=jax.ShapeDtypeStruct((B, P, Lp, W), bf16),
        scratch_shapes=[pltpu.VMEM((P, W, W), f32), pltpu.VMEM((P, rows, W), f32), pltpu.VMEM((P, rows, W), f32),
                        pltpu.VMEM((P, C, W), f32)],
        compiler_params=_params("parallel", "arbitrary"),
        name="hgrn2",
    )(qb, logf, kb, ib, gb, gain, tril, bones)


def _ffn_tail(h, mix, g1_ref, b1_ref, wg_ref, wu_ref, wd_ref, g2_ref, b2_ref, out_ref):
    h1 = _layer_norm(ALPHA * h + mix, g1_ref[...], b1_ref[...])
    hb = h1.astype(bf16)
    g = _dot(hb, wg_ref[...])
    u = _dot(hb, wu_ref[...])
    act = (g / (1.0 + jnp.exp(-g))) * u
    f = _dot(act.astype(bf16), wd_ref[...])
    out_ref[...] = _layer_norm(ALPHA * h1 + f, g2_ref[...], b2_ref[...])


def _post_even_kernel(h_ref, oa_ref, ob_ref, wout_ref, *rest):
    ob = jnp.concatenate([ob_ref[p] for p in range(B_WIDTH // LANES)], axis=1)
    mix = _dot(oa_ref[...], wout_ref[:A_WIDTH, :]) + _dot(ob, wout_ref[A_WIDTH:, :])
    _ffn_tail(h_ref[...], mix, *rest)


def _post_odd_kernel(h_ref, prev_ref, wpool_ref, pscale_ref, *rest, tm):
    i = pl.program_id(1)
    h = h_ref[...]
    prev = jnp.where(i == 0, 0.0, prev_ref[...])
    x = jnp.concatenate([prev, h], axis=0)
    pos = i * tm + lax.broadcasted_iota(jnp.int32, (tm, 1), 0)
    parts = []
    for g, win in enumerate(POOL_WINDOWS):
        sl = slice(g * POOL_GROUP_DIM, (g + 1) * POOL_GROUP_DIM)
        s = x[:, sl]
        span = 1
        while span < win:
            s = s + pltpu.roll(s, span, axis=0)
            span *= 2
        cnt = jnp.minimum(pos + 1, win).astype(f32)
        d = s[POOL_HALO:, :] / cnt - h[:, sl]
        parts.append(_dot(d.astype(bf16), wpool_ref[g]))
    mix = jnp.concatenate(parts, axis=-1) * pscale_ref[...]
    _ffn_tail(h, mix, *rest)


def _post(h, mixer_inputs, mixer_specs, kernel, tail_consts, *, tm):
    B, Lp, D = h.shape
    row = pl.BlockSpec((None, tm, D), lambda b, i: (b, i, 0))
    return pl.pallas_call(
        kernel,
        grid=(B, Lp // tm),
        in_specs=[row] + mixer_specs + [_const_spec(c.shape) for c in tail_consts],
        out_specs=row,
        out_shape=jax.ShapeDtypeStruct((B, Lp, D), f32),
        compiler_params=_params("parallel", "parallel"),
        name=getattr(kernel, "__name__", "post_odd"),
    )(h, *mixer_inputs, *tail_consts)


def _post_even(h, oa, ob, w_out, tail_consts, *, tm):
    row = lambda c: pl.BlockSpec((None, tm, c), lambda b, i: (b, i, 0))
    pairs = pl.BlockSpec((None, B_WIDTH // LANES, tm, LANES), lambda b, i: (b, 0, i, 0))
    return _post(h, [oa, ob, w_out], [row(A_WIDTH), pairs, _const_spec(w_out.shape)],
                 _post_even_kernel, tail_consts, tm=tm)


def _post_odd(h, w_pool, pool_scale, tail_consts, *, tm):
    D = h.shape[-1]
    halo_per_tile = tm // POOL_HALO
    prev = pl.BlockSpec((None, POOL_HALO, D), lambda b, i: (b, jnp.maximum(i * halo_per_tile - 1, 0), 0))
    kern = functools.partial(_post_odd_kernel, tm=tm)
    kern.__name__ = "post_odd"
    return _post(h, [h, w_pool, pool_scale], [prev, _const_spec(w_pool.shape), _const_spec(pool_scale.shape)],
                 kern, tail_consts, tm=tm)


_W_IN_SPLITS = np.cumsum([0, Q_RANK, A_WIDTH, A_WIDTH, IDX_DIM, IDX_HEADS, B_WIDTH, B_WIDTH, B_WIDTH, B_WIDTH])


def _w_in_seg(w, j):
    return w[:, _W_IN_SPLITS[j]:_W_IN_SPLITS[j + 1]]


def _pack_w_in(w):
    segs = [_w_in_seg(w, j) for j in (0, 1, 3, 3, 5, 6, 7, 8)]
    return jnp.concatenate(segs, axis=1).astype(bf16)


def kernel(x, meta_tokens, w_in, cq_gain, w_uq, w_uq_idx, kidx_gain, kidx_bias, lb_raw, onorm_gain, w_out,
           w_pool, pool_scale, ln_mix_g, ln_mix_b, w_gate, w_up, w_down, ln_ffn_g, ln_ffn_b):
    B, S, D = x.shape
    L = N_META + S
    Lp = -(-L // SEQ_TILE) * SEQ_TILE
    topk = min(TOPK_MAX, S // 4)
    tm = SEQ_TILE

    meta = jnp.broadcast_to(meta_tokens.astype(x.dtype)[None], (B, N_META, D))
    h = jnp.concatenate([meta, x, jnp.zeros((B, Lp - L, D), x.dtype)], axis=1)

    lower = jnp.cumsum(jax.nn.softmax(lb_raw.astype(f32), axis=0), axis=0)
    lower = lower - lower[:1]

    row = lambda v: v.reshape(1, -1).astype(f32)
    for layer in range(DEPTH):
        j = layer // 2
        tail = [row(ln_mix_g[layer]), row(ln_mix_b[layer]), w_gate[layer].astype(bf16), w_up[layer].astype(bf16),
                w_down[layer].astype(bf16), row(ln_ffn_g[layer]), row(ln_ffn_b[layer])]
        if layer % 2 == 0:
            lb = lower[j]
            qa, ka, vt, qi, klr, wit, qb, logf, kb, ib, gb = _inproj(
                h, _pack_w_in(w_in[j]), _w_in_seg(w_in[j], 2).T.astype(bf16), _w_in_seg(w_in[j], 4).T.astype(bf16),
                row(cq_gain[j]), w_uq[j].astype(bf16), w_uq_idx[j].astype(bf16),
                row(jnp.tile(kidx_gain[j], 2)), row(jnp.tile(kidx_bias[j], 2)),
                row(jnp.log(lb)), row(jnp.log1p(-lb)), row(1.0 - lb), tm=tm)
            oa = _dsa(qa, qi, wit, klr, ka, vt, topk=topk, tq=SEQ_TILE)
            ob = _hgrn2(qb, logf, kb, ib, gb, onorm_gain[j].astype(f32).reshape(-1, 1, LANES))
            h = _post_even(h, oa, ob, w_out[j].astype(bf16), tail, tm=tm)
        else:
            h = _post_odd(h, w_pool[j].astype(bf16), row(pool_scale[j]), tail, tm=tm)
    return h[:, N_META:L]
```

```python
import functools

import jax
import jax.numpy as jnp
import numpy as np
from jax import lax
from jax.experimental import pallas as pl
from jax.experimental.pallas import tpu as pltpu

D_MODEL = 1024
DEPTH = 4
N_META = 16
A_HEADS = 8
A_HEAD_DIM = 64
A_WIDTH = A_HEADS * A_HEAD_DIM
Q_RANK = 256
IDX_HEADS = 16
IDX_DIM = 64
TOPK_MAX = 256
B_HEADS = 8
B_KEY_DIM = 64
B_WIDTH = B_HEADS * B_KEY_DIM
POOL_WINDOWS = (2, 4, 8, 16)
POOL_GROUP_DIM = D_MODEL // len(POOL_WINDOWS)
D_FF = -(-8 * D_MODEL // (3 * 256)) * 256
ALPHA = (2 * DEPTH) ** 0.25
LN_EPS = 1e-5
LOG2E = 1.4426950408889634

LANES = 128
SUBLANES = 8
MXU_DIM = 256
VMEM_LIMIT = 56 * 1024 * 1024

SEQ_TILE = 256
HG_CHUNK = 64
POOL_HALO = 16
ONES_ROWS = 2 * SUBLANES

_SEG_A = (0, Q_RANK + A_WIDTH)
_SEG_K = (_SEG_A[1], _SEG_A[1] + LANES)
_SEG_B = (_SEG_K[1], _SEG_K[1] + 4 * B_WIDTH)
IN_COLS_PADDED = _SEG_B[1]

INT_MIN = -2 ** 31
NEG_BIG = -1e30

bf16 = jnp.bfloat16
f32 = jnp.float32


def _dot(a, b):
    return jnp.dot(a, b, preferred_element_type=f32)


def _dot_nt(a, b):
    return lax.dot_general(a, b, (((1,), (1,)), ((), ())), preferred_element_type=f32)


def _dot_tn(a, b):
    return lax.dot_general(a, b, (((0,), (0,)), ((), ())), preferred_element_type=f32)


def _layer_norm(x, g, b):
    mu = jnp.mean(x, axis=-1, keepdims=True)
    xc = x - mu
    var = jnp.mean(xc * xc, axis=-1, keepdims=True)
    return xc * lax.rsqrt(var + LN_EPS) * g + b


def _const_spec(shape):
    nd = len(shape)
    return pl.BlockSpec(shape, lambda *_: (0,) * nd, pipeline_mode=pl.Buffered(1))


def _params(*sem):
    return pltpu.CompilerParams(dimension_semantics=sem, vmem_limit_bytes=VMEM_LIMIT)


def _inproj_kernel(h_ref, w_ref, wvt_ref, wwt_ref, cqg_ref, wuq_ref, wuqi_ref, kig_ref, kib_ref,
                   loglb_ref, log1mlb_ref,
                   qa_ref, ka_ref, vt_ref, qi_ref, klr_ref, wit_ref, qb_ref, logf_ref, lk_ref, ib_ref, gb_ref):
    tm = h_ref.shape[0]
    hb = h_ref[...].astype(bf16)

    pa = _dot(hb, w_ref[:, _SEG_A[0]:_SEG_A[1]])
    cq = pa[:, :Q_RANK]
    c = cq * lax.rsqrt(jnp.mean(cq * cq, axis=-1, keepdims=True) + LN_EPS) * cqg_ref[...]
    cb = c.astype(bf16)
    qa_ref[...] = (_dot(cb, wuq_ref[...]) * (A_HEAD_DIM ** -0.5)).astype(bf16)
    qi_ref[...] = _dot(cb, wuqi_ref[...]).astype(bf16)
    ka_ref[...] = pa[:, Q_RANK:].astype(bf16)
    vt_ref[...] = _dot_nt(wvt_ref[...], hb).astype(bf16)
    wit_ref[...] = _dot_nt(wwt_ref[...], hb) * (IDX_HEADS ** -0.5 * IDX_DIM ** -0.5)

    kd = _dot(hb, w_ref[:, _SEG_K[0]:_SEG_K[1]])
    left = lax.broadcasted_iota(jnp.int32, (1, LANES), 1) < IDX_DIM
    mu = jnp.sum(jnp.where(left, kd, 0.0), axis=-1, keepdims=True) * (1.0 / IDX_DIM)
    xc = kd - mu
    var = jnp.sum(jnp.where(left, xc * xc, 0.0), axis=-1, keepdims=True) * (1.0 / IDX_DIM)
    kn = xc * lax.rsqrt(var + LN_EPS) * kig_ref[...] + kib_ref[...]
    klr_ref[:tm, :] = jnp.where(left, kn, 0.0).astype(bf16)
    klr_ref[tm:, :] = jnp.where(left, 0.0, kn).astype(bf16)

    pb = _dot(hb, w_ref[:, _SEG_B[0]:_SEG_B[1]])
    z = pb[:, B_WIDTH:2 * B_WIDTH]
    e = jnp.exp(-jnp.abs(z))
    log_sig = jnp.minimum(z, 0.0) - jnp.log1p(e)
    a = loglb_ref[...]
    b = log1mlb_ref[...] + log_sig
    logf = jnp.maximum(a, b) + jnp.log1p(jnp.exp(-jnp.abs(a - b)))
    logk = log1mlb_ref[...] + (log_sig - z)
    for p in range(B_WIDTH // LANES):
        sl = slice(p * LANES, (p + 1) * LANES)
        qb_ref[p] = pb[:, sl]
        logf_ref[p] = logf[:, sl]
        lk_ref[p] = logk[:, sl]
        ib_ref[p] = pb[:, 2 * B_WIDTH + p * LANES:2 * B_WIDTH + (p + 1) * LANES]
        gb_ref[p] = pb[:, 3 * B_WIDTH + p * LANES:3 * B_WIDTH + (p + 1) * LANES]


def _inproj(h, w_cat, w_vt, w_wt, cq_gain, w_uq, w_uq_idx, kig2, kib2, loglb, log1mlb, *, tm):
    B, Lp, D = h.shape
    nt = Lp // tm
    row = lambda c: pl.BlockSpec((None, tm, c), lambda b, i: (b, i, 0))
    tile = lambda r, c: pl.BlockSpec((None, None, r, c), lambda b, i: (b, i, 0, 0))
    tok = lambda c, dt: (row(c), jax.ShapeDtypeStruct((B, Lp, c), dt))
    til = lambda r, c, dt: (tile(r, c), jax.ShapeDtypeStruct((B, nt, r, c), dt))
    npair = B_WIDTH // LANES
    pair = (pl.BlockSpec((None, npair, tm, LANES), lambda b, i: (b, 0, i, 0)),
            jax.ShapeDtypeStruct((B, npair, Lp, LANES), f32))
    outs = [tok(A_WIDTH, bf16), tok(A_WIDTH, bf16), til(A_WIDTH, tm, bf16), tok(IDX_HEADS * IDX_DIM, bf16),
            til(2 * tm, LANES, bf16), til(IDX_HEADS, tm, f32),
            pair, pair, pair, pair, pair]
    consts = [w_cat, w_vt, w_wt, cq_gain, w_uq, w_uq_idx, kig2, kib2, loglb, log1mlb]
    return pl.pallas_call(
        _inproj_kernel,
        grid=(B, nt),
        in_specs=[row(D)] + [_const_spec(c.shape) for c in consts],
        out_specs=[s for s, _ in outs],
        out_shape=[o for _, o in outs],
        compiler_params=_params("parallel", "parallel"),
        name="inproj",
    )(h, *consts)


def _tree(op, xs):
    xs = list(xs)
    while len(xs) > 1:
        xs = [op(xs[j], xs[j + 1]) for j in range(0, len(xs) - 1, 2)] + ([xs[-1]] if len(xs) % 2 else [])
    return xs[0]


def _f32_to_key(x):
    bits = pltpu.bitcast(x, jnp.int32)
    return bits ^ ((bits >> 31) & jnp.int32(0x7FFFFFFF))


def _pair_loop(n, body, carry):
    carry = lax.fori_loop(0, n // 2, lambda j, c: body(2 * j + 1, body(2 * j, c)), carry)
    return lax.fori_loop(2 * (n // 2), n, body, carry)


def _dsa_kernel(qa_ref, qi_ref, wit_ref, klr_ref, ka_ref, vt_ref, pos_ref, o_ref,
                keys_scr, half_scr, bias_scr, sacc_scr, qaug_scr, pm_scr, acc_scr, out_scr, *, topk, tq):
    i = pl.program_id(1)
    nkt = i + 1
    lane = lax.broadcasted_iota(jnp.int32, (1, LANES), 1)
    left = lane < A_HEAD_DIM
    q_pos = i * tq + lax.broadcasted_iota(jnp.int32, (1, tq), 1)
    k_iota = lax.broadcasted_iota(jnp.int32, (tq, 1), 0)

    def alibi_coef(h):
        slope = 2.0 ** -(h + 1)
        return jnp.where(lane == 0, slope * tq, jnp.where(lane == 1, slope, 0.0))

    for h in range(A_HEADS):
        qp = qa_ref[:, (h // 2) * LANES:(h // 2 + 1) * LANES]
        keep = left if h % 2 == 0 else jnp.logical_not(left)
        qaug_scr[h, :, :LANES] = jnp.where(keep, qp, jnp.zeros_like(qp))
        qaug_scr[h, :, LANES:] = jnp.broadcast_to(alibi_coef(h).astype(bf16), (tq, LANES))

    def score_body(kt, carry):
        klr = klr_ref[kt]
        for p in range(IDX_HEADS // 2):
            s = _dot_nt(klr, qi_ref[:, p * LANES:(p + 1) * LANES])
            t = (jnp.maximum(s[:tq], 0.0) * wit_ref[2 * p:2 * p + 1, :]
                 + jnp.maximum(s[tq:], 0.0) * wit_ref[2 * p + 1:2 * p + 2, :])
            if p == 0:
                sacc_scr[...] = t
            else:
                sacc_scr[...] += t
        causal = (kt * tq + k_iota) <= q_pos
        keys_scr[kt] = jnp.where(causal, _f32_to_key(sacc_scr[...]), jnp.int32(INT_MIN))
        return carry

    _pair_loop(nkt, score_body, 0)

    pack = 2 * SUBLANES

    def kth_largest_16(src_scr):
        def bit_body(it, c):
            trial = c + jnp.left_shift(jnp.int32(1), 15 - it)
            trial_b = jnp.broadcast_to(trial, (pack, tq)).astype(jnp.int16)

            def cnt_body(kt, a):
                part = _tree(jnp.add, ((src_scr[kt, r * pack:(r + 1) * pack, :] >= trial_b).astype(jnp.int16)
                                       for r in range(tq // pack)))
                return a + part.astype(jnp.int32)

            a = lax.fori_loop(0, nkt, cnt_body, jnp.zeros((pack, tq), jnp.int32))
            return jnp.where(jnp.sum(a, axis=0, keepdims=True) >= topk, trial, c)

        return lax.fori_loop(0, 16, bit_body, jnp.full((1, tq), -2 ** 15, jnp.int32))

    def high_body(kt, carry):
        half_scr[kt] = (keys_scr[kt] >> 16).astype(jnp.int16)
        return carry

    lax.fori_loop(0, nkt, high_body, 0)
    p_hi = kth_largest_16(half_scr)

    def low_body(kt, carry):
        key = keys_scr[kt]
        hi = key >> 16
        lo = (key & jnp.int32(0xFFFF)) - 2 ** 15
        val = jnp.where(hi == p_hi, lo, jnp.where(hi > p_hi, 2 ** 15 - 1, -2 ** 15))
        half_scr[kt] = val.astype(jnp.int16)
        return carry

    lax.fori_loop(0, nkt, low_body, 0)
    p_lo = kth_largest_16(half_scr)
    thr = jnp.maximum((p_hi << 16) + (p_lo + 2 ** 15), jnp.int32(INT_MIN + 1))

    def bias_body(kt, carry):
        bias_scr[kt] = jnp.where(keys_scr[kt] >= thr, 0.0, NEG_BIG)
        return carry

    lax.fori_loop(0, nkt, bias_body, 0)

    groups = tq // SUBLANES
    row_groups = lambda x: (x[r * SUBLANES:(r + 1) * SUBLANES] for r in range(groups))

    def logits(kt, h):
        k0 = pl.multiple_of(kt * tq, tq)
        kaug = jnp.concatenate([ka_ref[pl.ds(k0, tq), (h // 2) * LANES:(h // 2 + 1) * LANES],
                                pos_ref[pl.ds(k0, tq), :]], axis=1)
        return _dot_nt(kaug, qaug_scr[h]) + bias_scr[kt]

    def max_body(kt, m8):
        return tuple(jnp.maximum(m8[h], _tree(jnp.maximum, row_groups(logits(kt, h)))) for h in range(A_HEADS))

    m8 = _pair_loop(nkt, max_body, (jnp.full((SUBLANES, tq), NEG_BIG, f32),) * A_HEADS)

    m_cols = jnp.concatenate([jnp.max(m8[h], axis=0, keepdims=True) for h in range(A_HEADS)], axis=0).T
    for h in range(A_HEADS):
        m_col = m_cols[:, h:h + 1]
        m_hi = m_col.astype(bf16).astype(f32)
        coef = jnp.where(lane == 2, -m_hi, jnp.where(lane == 3, m_hi - m_col, alibi_coef(h)))
        qaug_scr[h, :, LANES:] = coef.astype(bf16)

    acc_scr[...] = jnp.zeros(acc_scr.shape, f32)
    ones = jnp.ones((ONES_ROWS, tq), bf16)

    def acc_body(kt, carry):
        for h in range(A_HEADS):
            pm_scr[h] = jnp.exp(logits(kt, h)).astype(bf16)
        for h in range(A_HEADS):
            rows = slice(h * (A_HEAD_DIM + ONES_ROWS), (h + 1) * (A_HEAD_DIM + ONES_ROWS))
            v_ones = jnp.concatenate([vt_ref[kt, h * A_HEAD_DIM:(h + 1) * A_HEAD_DIM, :], ones], axis=0)
            acc_scr[rows, :] += _dot(v_ones, pm_scr[h])
        return carry

    _pair_loop(nkt, acc_body, 0)
    for h in range(A_HEADS):
        r0 = h * (A_HEAD_DIM + ONES_ROWS)
        out_scr[h * A_HEAD_DIM:(h + 1) * A_HEAD_DIM, :] = (acc_scr[r0:r0 + A_HEAD_DIM, :]
                                                           / acc_scr[r0 + A_HEAD_DIM:r0 + A_HEAD_DIM + 1, :])
    o_ref[...] = out_scr[...].T.astype(o_ref.dtype)


def _dsa(qa, qi, wit, klr, ka, vt, *, topk, tq):
    B, Lp, _ = qa.shape
    nq = Lp // tq
    kpos = np.arange(Lp)
    posfeat = np.zeros((Lp, LANES), np.float32)
    posfeat[:, 0] = kpos // tq
    posfeat[:, 1] = kpos % tq
    posfeat[:, 2:4] = 1.0
    posfeat = jnp.asarray(posfeat, bf16)
    qrow = lambda c: pl.BlockSpec((None, tq, c), lambda b, i: (b, i, 0))
    seq = lambda *s: pl.BlockSpec((None,) + s, lambda b, i: (b,) + (0,) * len(s), pipeline_mode=pl.Buffered(1))
    return pl.pallas_call(
        functools.partial(_dsa_kernel, topk=topk, tq=tq),
        grid=(B, nq),
        in_specs=[qrow(A_WIDTH), qrow(IDX_HEADS * IDX_DIM),
                  pl.BlockSpec((None, None, IDX_HEADS, tq), lambda b, i: (b, i, 0, 0)),
                  seq(nq, 2 * tq, LANES), seq(Lp, A_WIDTH), seq(nq, A_WIDTH, tq), _const_spec(posfeat.shape)],
        out_specs=qrow(A_WIDTH),
        out_shape=jax.ShapeDtypeStruct((B, Lp, A_WIDTH), bf16),
        scratch_shapes=[
            pltpu.VMEM((nq, tq, tq), jnp.int32),
            pltpu.VMEM((nq, tq, tq), jnp.int16),
            pltpu.VMEM((nq, tq, tq), f32),
            pltpu.VMEM((tq, tq), f32),
            pltpu.VMEM((A_HEADS, tq, 2 * LANES), bf16),
            pltpu.VMEM((A_HEADS, tq, tq), bf16),
            pltpu.VMEM((A_HEADS * (A_HEAD_DIM + ONES_ROWS), tq), f32),
            pltpu.VMEM((A_WIDTH, tq), f32),
        ],
        compiler_params=_params("parallel", "arbitrary"),
        name="dsa",
    )(qa, qi, wit, klr, ka, vt, posfeat)


def _hgrn2_offsets(C):
    ns = [SUBLANES * (t // SUBLANES + 1) for t in range(C)]
    return ns, np.concatenate([[0], np.cumsum(ns)]).tolist()


def _hgrn2_kernel(qb_ref, logf_ref, lk_ref, ib_ref, gb_ref, gain_ref, tril_ref, bones_ref, o_ref,
                  st_scr, a_scr, w_scr, intra_scr):
    C = HG_CHUNK
    P = qb_ref.shape[0]
    pairs = range(P)
    ns, off = _hgrn2_offsets(C)

    @pl.when(pl.program_id(1) == 0)
    def _():
        st_scr[...] = jnp.zeros(st_scr.shape, f32)

    tril = tril_ref[...]
    bones = bones_ref[...]

    def cumsum(lf):
        hi = lf.astype(bf16)
        r1 = lf - hi.astype(f32)
        mid = r1.astype(bf16)
        lo = (r1 - mid.astype(f32)).astype(bf16)
        return _dot(tril, hi) + _dot(tril, mid) + _dot(tril, lo)

    b2 = [cumsum(logf_ref[p]) * LOG2E for p in pairs]
    c2 = [b2[p] - lk_ref[p] * LOG2E for p in pairs]
    q = [qb_ref[p] for p in pairs]
    v = [ib_ref[p] for p in pairs]

    inter = [_dot_nt((q[p] * jnp.exp2(b2[p])).astype(bf16), st_scr[p].astype(bf16)) for p in pairs]
    for p in pairs:
        b_last = b2[p][C - 1:C, :]
        k_dec = jnp.exp2(b_last - c2[p]).astype(bf16)
        upd = _dot_tn(v[p].astype(bf16), k_dec)
        st_scr[p] = st_scr[p] * jnp.exp2(b_last) + upd * bones.astype(f32)

    sidx = lax.broadcasted_iota(jnp.int32, (SUBLANES, 1), 0)
    for p in pairs:
        for t in range(C):
            n = ns[t]
            diff = b2[p][t:t + 1, :] - c2[p][:n]
            last = jnp.where(sidx <= t % SUBLANES, diff[n - SUBLANES:], -jnp.inf)
            diff = last if n == SUBLANES else jnp.concatenate([diff[:n - SUBLANES], last], axis=0)
            a_scr[p, off[t]:off[t] + n, :] = q[p][t:t + 1, :] * jnp.exp2(diff)
    for p in pairs:
        w_scr[p] = _dot(a_scr[p].astype(bf16), bones)
    for p in pairs:
        for t in range(C):
            n = ns[t]
            intra_scr[p, t:t + 1, :] = jnp.sum(w_scr[p, off[t]:off[t] + n, :] * v[p][:n], axis=0, keepdims=True)

    o = [inter[p] + intra_scr[p] for p in pairs]
    ss = []
    for p in pairs:
        o2 = o[p] * o[p]
        o2_hi = o2.astype(bf16)
        o2_lo = (o2 - o2_hi.astype(f32)).astype(bf16)
        ss.append(_dot(o2_hi, bones) + _dot(o2_lo, bones))
    for p in pairs:
        on = o[p] * lax.rsqrt(ss[p] * (1.0 / B_KEY_DIM) + LN_EPS) * gain_ref[p]
        g = gb_ref[p]
        o_ref[p] = (on * (g / (1.0 + jnp.exp(-g)))).astype(o_ref.dtype)


def _hgrn2(qb, logf, logk, ib, gb, gain):
    B, P, Lp, W = qb.shape
    C = HG_CHUNK
    head = np.arange(W) // B_KEY_DIM
    tril = jnp.asarray(np.tril(np.ones((C, C), np.float32)), bf16)
    bones = jnp.asarray((head[:, None] == head[None, :]).astype(np.float32), bf16)
    rows = _hgrn2_offsets(C)[1][-1]
    row = pl.BlockSpec((None, P, C, W), lambda b, c: (b, 0, c, 0))
    return pl.pallas_call(
        _hgrn2_kernel,
        grid=(B, Lp // C),
        in_specs=[row] * 5 + [_const_spec(gain.shape), _const_spec(tril.shape), _const_spec(bones.shape)],
        out_specs=row,
        out_shape=jax.ShapeDtypeStruct((B, P, Lp, W), bf16),
        scratch_shapes=[pltpu.VMEM((P, W, W), f32), pltpu.VMEM((P, rows, W), f32), pltpu.VMEM((P, rows, W), f32),
                        pltpu.VMEM((P, C, W), f32)],
        compiler_params=_params("parallel", "arbitrary"),
        name="hgrn2",
    )(qb, logf, logk, ib, gb, gain, tril, bones)


def _ffn_tail(h, mix, g1_ref, b1_ref, wg_ref, wu_ref, wd_ref, g2_ref, b2_ref, out_ref):
    h1 = _layer_norm(ALPHA * h + mix, g1_ref[...], b1_ref[...])
    hb = h1.astype(bf16)
    g = _dot(hb, wg_ref[...])
    u = _dot(hb, wu_ref[...])
    act = (g / (1.0 + jnp.exp(-g))) * u
    f = _dot(act.astype(bf16), wd_ref[...])
    out_ref[...] = _layer_norm(ALPHA * h1 + f, g2_ref[...], b2_ref[...])


def _post_even_kernel(h_ref, oa_ref, ob_ref, wout_ref, *rest):
    ob = jnp.concatenate([ob_ref[p] for p in range(B_WIDTH // LANES)], axis=1)
    mix = _dot(oa_ref[...], wout_ref[:A_WIDTH, :]) + _dot(ob, wout_ref[A_WIDTH:, :])
    _ffn_tail(h_ref[...], mix, *rest)


def _post_odd_kernel(h_ref, prev_ref, wpool_ref, pscale_ref, *rest, tm):
    i = pl.program_id(1)
    h = h_ref[...]
    prev = jnp.where(i == 0, 0.0, prev_ref[...])
    x = jnp.concatenate([prev, h], axis=0)
    pos = i * tm + lax.broadcasted_iota(jnp.int32, (tm, 1), 0)
    parts = []
    for g, win in enumerate(POOL_WINDOWS):
        sl = slice(g * POOL_GROUP_DIM, (g + 1) * POOL_GROUP_DIM)
        s = x[:, sl]
        span = 1
        while span < win:
            s = s + pltpu.roll(s, span, axis=0)
            span *= 2
        cnt = jnp.minimum(pos + 1, win).astype(f32)
        d = s[POOL_HALO:, :] / cnt - h[:, sl]
        parts.append(_dot(d.astype(bf16), wpool_ref[g]))
    mix = jnp.concatenate(parts, axis=-1) * pscale_ref[...]
    _ffn_tail(h, mix, *rest)


def _post(h, mixer_inputs, mixer_specs, kernel, tail_consts, *, tm):
    B, Lp, D = h.shape
    row = pl.BlockSpec((None, tm, D), lambda b, i: (b, i, 0))
    return pl.pallas_call(
        kernel,
        grid=(B, Lp // tm),
        in_specs=[row] + mixer_specs + [_const_spec(c.shape) for c in tail_consts],
        out_specs=row,
        out_shape=jax.ShapeDtypeStruct((B, Lp, D), f32),
        compiler_params=_params("parallel", "parallel"),
        name=getattr(kernel, "__name__", "post_odd"),
    )(h, *mixer_inputs, *tail_consts)


def _post_even(h, oa, ob, w_out, tail_consts, *, tm):
    row = lambda c: pl.BlockSpec((None, tm, c), lambda b, i: (b, i, 0))
    pairs = pl.BlockSpec((None, B_WIDTH // LANES, tm, LANES), lambda b, i: (b, 0, i, 0))
    return _post(h, [oa, ob, w_out], [row(A_WIDTH), pairs, _const_spec(w_out.shape)],
                 _post_even_kernel, tail_consts, tm=tm)


def _post_odd(h, w_pool, pool_scale, tail_consts, *, tm):
    D = h.shape[-1]
    halo_per_tile = tm // POOL_HALO
    prev = pl.BlockSpec((None, POOL_HALO, D), lambda b, i: (b, jnp.maximum(i * halo_per_tile - 1, 0), 0))
    kern = functools.partial(_post_odd_kernel, tm=tm)
    kern.__name__ = "post_odd"
    return _post(h, [h, w_pool, pool_scale], [prev, _const_spec(w_pool.shape), _const_spec(pool_scale.shape)],
                 kern, tail_consts, tm=tm)


_W_IN_SPLITS = np.cumsum([0, Q_RANK, A_WIDTH, A_WIDTH, IDX_DIM, IDX_HEADS, B_WIDTH, B_WIDTH, B_WIDTH, B_WIDTH])


def _w_in_seg(w, j):
    return w[:, _W_IN_SPLITS[j]:_W_IN_SPLITS[j + 1]]


def _pack_w_in(w):
    segs = [_w_in_seg(w, j) for j in (0, 1, 3, 3, 5, 6, 7, 8)]
    return jnp.concatenate(segs, axis=1).astype(bf16)


def kernel(x, meta_tokens, w_in, cq_gain, w_uq, w_uq_idx, kidx_gain, kidx_bias, lb_raw, onorm_gain, w_out,
           w_pool, pool_scale, ln_mix_g, ln_mix_b, w_gate, w_up, w_down, ln_ffn_g, ln_ffn_b):
    B, S, D = x.shape
    L = N_META + S
    Lp = -(-L // SEQ_TILE) * SEQ_TILE
    topk = min(TOPK_MAX, S // 4)
    tm = SEQ_TILE

    meta = jnp.broadcast_to(meta_tokens.astype(x.dtype)[None], (B, N_META, D))
    h = jnp.concatenate([meta, x, jnp.zeros((B, Lp - L, D), x.dtype)], axis=1)

    lower = jnp.cumsum(jax.nn.softmax(lb_raw.astype(f32), axis=0), axis=0)
    lower = lower - lower[:1]

    row = lambda v: v.reshape(1, -1).astype(f32)
    for layer in range(DEPTH):
        j = layer // 2
        tail = [row(ln_mix_g[layer]), row(ln_mix_b[layer]), w_gate[layer].astype(bf16), w_up[layer].astype(bf16),
                w_down[layer].astype(bf16), row(ln_ffn_g[layer]), row(ln_ffn_b[layer])]
        if layer % 2 == 0:
            lb = lower[j]
            qa, ka, vt, qi, klr, wit, qb, logf, logk, ib, gb = _inproj(
                h, _pack_w_in(w_in[j]), _w_in_seg(w_in[j], 2).T.astype(bf16), _w_in_seg(w_in[j], 4).T.astype(bf16),
                row(cq_gain[j]), w_uq[j].astype(bf16), w_uq_idx[j].astype(bf16),
                row(jnp.tile(kidx_gain[j], 2)), row(jnp.tile(kidx_bias[j], 2)),
                row(jnp.log(lb)), row(jnp.log1p(-lb)), tm=tm)
            oa = _dsa(qa, qi, wit, klr, ka, vt, topk=topk, tq=SEQ_TILE)
            ob = _hgrn2(qb, logf, logk, ib, gb, onorm_gain[j].astype(f32).reshape(-1, 1, LANES))
            h = _post_even(h, oa, ob, w_out[j].astype(bf16), tail, tm=tm)
        else:
            h = _post_odd(h, w_pool[j].astype(bf16), row(pool_scale[j]), tail, tm=tm)
    return h[:, N_META:L]
```

```python
import functools

import jax
import jax.numpy as jnp
import numpy as np
from jax import lax
from jax.experimental import pallas as pl
from jax.experimental.pallas import tpu as pltpu

D_MODEL = 1024
DEPTH = 4
N_META = 16
A_HEADS = 8
A_HEAD_DIM = 64
A_WIDTH = A_HEADS * A_HEAD_DIM
Q_RANK = 256
IDX_HEADS = 16
IDX_DIM = 64
TOPK_MAX = 256
B_HEADS = 8
B_KEY_DIM = 64
B_WIDTH = B_HEADS * B_KEY_DIM
POOL_WINDOWS = (2, 4, 8, 16)
POOL_GROUP_DIM = D_MODEL // len(POOL_WINDOWS)
D_FF = -(-8 * D_MODEL // (3 * 256)) * 256
ALPHA = (2 * DEPTH) ** 0.25
LN_EPS = 1e-5
LOG2E = 1.4426950408889634

LANES = 128
SUBLANES = 8
MXU_DIM = 256
VMEM_LIMIT = 56 * 1024 * 1024

SEQ_TILE = 256
HG_CHUNK = 128
HG_SUB = 16
POOL_HALO = 16
ONES_ROWS = 2 * SUBLANES
DIGIT_BITS = 16
DIGIT_DTYPE = jnp.int16

_SEG_A = (0, Q_RANK + A_WIDTH)
_SEG_K = (_SEG_A[1], _SEG_A[1] + LANES)
_SEG_B = (_SEG_K[1], _SEG_K[1] + 4 * B_WIDTH)
IN_COLS_PADDED = _SEG_B[1]

INT_MIN = -2 ** 31
NEG_BIG = -1e30

bf16 = jnp.bfloat16
f32 = jnp.float32


def _dot(a, b):
    return jnp.dot(a, b, preferred_element_type=f32)


def _dot_nt(a, b):
    return lax.dot_general(a, b, (((1,), (1,)), ((), ())), preferred_element_type=f32)


def _dot_tn(a, b):
    return lax.dot_general(a, b, (((0,), (0,)), ((), ())), preferred_element_type=f32)


def _layer_norm(x, g, b):
    mu = jnp.mean(x, axis=-1, keepdims=True)
    xc = x - mu
    var = jnp.mean(xc * xc, axis=-1, keepdims=True)
    return xc * lax.rsqrt(var + LN_EPS) * g + b


def _const_spec(shape):
    nd = len(shape)
    return pl.BlockSpec(shape, lambda *_: (0,) * nd, pipeline_mode=pl.Buffered(1))


def _params(*sem):
    return pltpu.CompilerParams(dimension_semantics=sem, vmem_limit_bytes=VMEM_LIMIT)


def _inproj_kernel(h_ref, w_ref, wvt_ref, wwt_ref, cqg_ref, wuq_ref, wuqi_ref, kig_ref, kib_ref,
                   loglb_ref, log1mlb_ref,
                   qa_ref, ka_ref, vt_ref, qi_ref, klr_ref, wit_ref, qb_ref, logf_ref, lk_ref, ib_ref, gb_ref):
    tm = h_ref.shape[0]
    hb = h_ref[...].astype(bf16)

    pa = _dot(hb, w_ref[:, _SEG_A[0]:_SEG_A[1]])
    cq = pa[:, :Q_RANK]
    c = cq * lax.rsqrt(jnp.mean(cq * cq, axis=-1, keepdims=True) + LN_EPS) * cqg_ref[...]
    cb = c.astype(bf16)
    qa_ref[...] = (_dot(cb, wuq_ref[...]) * (A_HEAD_DIM ** -0.5)).astype(bf16)
    qi_ref[...] = _dot(cb, wuqi_ref[...]).astype(bf16)
    ka_ref[...] = pa[:, Q_RANK:].astype(bf16)
    vt_ref[...] = _dot_nt(wvt_ref[...], hb).astype(bf16)
    wit_ref[...] = _dot_nt(wwt_ref[...], hb) * (IDX_HEADS ** -0.5 * IDX_DIM ** -0.5)

    kd = _dot(hb, w_ref[:, _SEG_K[0]:_SEG_K[1]])
    left = lax.broadcasted_iota(jnp.int32, (1, LANES), 1) < IDX_DIM
    mu = jnp.sum(jnp.where(left, kd, 0.0), axis=-1, keepdims=True) * (1.0 / IDX_DIM)
    xc = kd - mu
    var = jnp.sum(jnp.where(left, xc * xc, 0.0), axis=-1, keepdims=True) * (1.0 / IDX_DIM)
    kn = xc * lax.rsqrt(var + LN_EPS) * kig_ref[...] + kib_ref[...]
    klr_ref[:tm, :] = jnp.where(left, kn, 0.0).astype(bf16)
    klr_ref[tm:, :] = jnp.where(left, 0.0, kn).astype(bf16)

    pb = _dot(hb, w_ref[:, _SEG_B[0]:_SEG_B[1]])
    z = pb[:, B_WIDTH:2 * B_WIDTH]
    e = jnp.exp(-jnp.abs(z))
    log_sig = jnp.minimum(z, 0.0) - jnp.log1p(e)
    a = loglb_ref[...]
    b = log1mlb_ref[...] + log_sig
    logf = jnp.maximum(a, b) + jnp.log1p(jnp.exp(-jnp.abs(a - b)))
    logk = log1mlb_ref[...] + (log_sig - z)
    for p in range(B_WIDTH // LANES):
        sl = slice(p * LANES, (p + 1) * LANES)
        qb_ref[p] = pb[:, sl]
        logf_ref[p] = logf[:, sl]
        lk_ref[p] = logk[:, sl]
        ib_ref[p] = pb[:, 2 * B_WIDTH + p * LANES:2 * B_WIDTH + (p + 1) * LANES]
        gb_ref[p] = pb[:, 3 * B_WIDTH + p * LANES:3 * B_WIDTH + (p + 1) * LANES]


def _inproj(h, w_cat, w_vt, w_wt, cq_gain, w_uq, w_uq_idx, kig2, kib2, loglb, log1mlb, *, tm):
    B, Lp, D = h.shape
    nt = Lp // tm
    row = lambda c: pl.BlockSpec((None, tm, c), lambda b, i: (b, i, 0))
    tile = lambda r, c: pl.BlockSpec((None, None, r, c), lambda b, i: (b, i, 0, 0))
    tok = lambda c, dt: (row(c), jax.ShapeDtypeStruct((B, Lp, c), dt))
    til = lambda r, c, dt: (tile(r, c), jax.ShapeDtypeStruct((B, nt, r, c), dt))
    npair = B_WIDTH // LANES
    pair = (pl.BlockSpec((None, npair, tm, LANES), lambda b, i: (b, 0, i, 0)),
            jax.ShapeDtypeStruct((B, npair, Lp, LANES), f32))
    outs = [tok(A_WIDTH, bf16), tok(A_WIDTH, bf16), til(A_WIDTH, tm, bf16), tok(IDX_HEADS * IDX_DIM, bf16),
            til(2 * tm, LANES, bf16), til(IDX_HEADS, tm, f32),
            pair, pair, pair, pair, pair]
    consts = [w_cat, w_vt, w_wt, cq_gain, w_uq, w_uq_idx, kig2, kib2, loglb, log1mlb]
    return pl.pallas_call(
        _inproj_kernel,
        grid=(B, nt),
        in_specs=[row(D)] + [_const_spec(c.shape) for c in consts],
        out_specs=[s for s, _ in outs],
        out_shape=[o for _, o in outs],
        compiler_params=_params("parallel", "parallel"),
        name="inproj",
    )(h, *consts)


def _tree(op, xs):
    xs = list(xs)
    while len(xs) > 1:
        xs = [op(xs[j], xs[j + 1]) for j in range(0, len(xs) - 1, 2)] + ([xs[-1]] if len(xs) % 2 else [])
    return xs[0]


def _f32_to_key(x):
    bits = pltpu.bitcast(x, jnp.int32)
    return bits ^ ((bits >> 31) & jnp.int32(0x7FFFFFFF))


TILE_UNROLLS = (4, 2, 1)


def _pair_loop(n, body, carry):
    start = 0
    for unroll in TILE_UNROLLS:
        def trip(j, c, start=start, unroll=unroll):
            for u in range(unroll):
                c = body(start + unroll * j + u, c)
            return c

        trips = (n - start) // unroll
        carry = lax.fori_loop(0, trips, trip, carry)
        start = start + unroll * trips
    return carry


def _dsa_kernel(qa_ref, qi_ref, wit_ref, klr_ref, ka_ref, vt_ref, pos_ref, o_ref,
                keys_scr, digit_scr, bias_scr, sacc_scr, qaug_scr, pm_scr, acc_scr, out_scr, *, topk, tq):
    i = pl.program_id(1)
    nkt = i + 1
    lane = lax.broadcasted_iota(jnp.int32, (1, LANES), 1)
    left = lane < A_HEAD_DIM
    q_pos = i * tq + lax.broadcasted_iota(jnp.int32, (1, tq), 1)
    k_iota = lax.broadcasted_iota(jnp.int32, (tq, 1), 0)

    def alibi_coef(h):
        slope = 2.0 ** -(h + 1)
        return jnp.where(lane == 0, slope * tq, jnp.where(lane == 1, slope, 0.0))

    for h in range(A_HEADS):
        qp = qa_ref[:, (h // 2) * LANES:(h // 2 + 1) * LANES]
        keep = left if h % 2 == 0 else jnp.logical_not(left)
        qaug_scr[h, :, :LANES] = jnp.where(keep, qp, jnp.zeros_like(qp))
        qaug_scr[h, :, LANES:] = jnp.broadcast_to(alibi_coef(h).astype(bf16), (tq, LANES))

    def score_body(kt, carry):
        klr = klr_ref[kt]
        for p in range(IDX_HEADS // 2):
            s = _dot_nt(klr, qi_ref[:, p * LANES:(p + 1) * LANES])
            t = (jnp.maximum(s[:tq], 0.0) * wit_ref[2 * p:2 * p + 1, :]
                 + jnp.maximum(s[tq:], 0.0) * wit_ref[2 * p + 1:2 * p + 2, :])
            if p == 0:
                sacc_scr[...] = t
            else:
                sacc_scr[...] += t
        causal = (kt * tq + k_iota) <= q_pos
        keys_scr[kt] = jnp.where(causal, _f32_to_key(sacc_scr[...]), jnp.int32(INT_MIN))
        return carry

    _pair_loop(nkt, score_body, 0)

    w = DIGIT_BITS
    pack = SUBLANES * (32 // w)
    half = 2 ** (w - 1)

    def kth_largest_digit():
        def bit_body(it, c):
            trial = c + jnp.left_shift(jnp.int32(1), w - 1 - it)
            trial_b = jnp.broadcast_to(trial, (pack, tq)).astype(DIGIT_DTYPE)

            def cnt_body(kt, a):
                return a + _tree(jnp.add, ((digit_scr[kt, r * pack:(r + 1) * pack, :] >= trial_b).astype(DIGIT_DTYPE)
                                           for r in range(tq // pack)))

            a = lax.fori_loop(0, nkt, cnt_body, jnp.zeros((pack, tq), DIGIT_DTYPE))
            return jnp.where(jnp.sum(a.astype(jnp.int32), axis=0, keepdims=True) >= topk, trial, c)

        return lax.fori_loop(0, w, bit_body, jnp.full((1, tq), -half, jnp.int32))

    prefix = None
    for stage in range(32 // w):
        shift = 32 - (stage + 1) * w

        def digit_body(kt, carry):
            top = keys_scr[kt] >> shift
            if prefix is None:
                val = top
            else:
                hi = top >> w
                val = jnp.where(hi == prefix, (top & (2 * half - 1)) - half,
                                jnp.where(hi > prefix, half - 1, -half))
            digit_scr[kt] = val.astype(DIGIT_DTYPE)
            return carry

        lax.fori_loop(0, nkt, digit_body, 0)
        digit = kth_largest_digit()
        prefix = digit if prefix is None else (prefix << w) + (digit + half)
    thr = jnp.maximum(prefix, jnp.int32(INT_MIN + 1))

    def bias_body(kt, carry):
        bias_scr[kt] = jnp.where(keys_scr[kt] >= thr, 0.0, NEG_BIG)
        return carry

    lax.fori_loop(0, nkt, bias_body, 0)

    groups = tq // SUBLANES
    row_groups = lambda x: (x[r * SUBLANES:(r + 1) * SUBLANES] for r in range(groups))

    def logits(kt, h):
        k0 = pl.multiple_of(kt * tq, tq)
        kaug = jnp.concatenate([ka_ref[pl.ds(k0, tq), (h // 2) * LANES:(h // 2 + 1) * LANES],
                                pos_ref[pl.ds(k0, tq), :]], axis=1)
        return _dot_nt(kaug, qaug_scr[h]) + bias_scr[kt]

    def max_body(kt, m8):
        return tuple(jnp.maximum(m8[h], _tree(jnp.maximum, row_groups(logits(kt, h)))) for h in range(A_HEADS))

    m8 = _pair_loop(nkt, max_body, (jnp.full((SUBLANES, tq), NEG_BIG, f32),) * A_HEADS)

    m_cols = jnp.concatenate([jnp.max(m8[h], axis=0, keepdims=True) for h in range(A_HEADS)], axis=0).T
    for h in range(A_HEADS):
        m_col = m_cols[:, h:h + 1]
        m_hi = m_col.astype(bf16).astype(f32)
        coef = jnp.where(lane == 2, -m_hi, jnp.where(lane == 3, m_hi - m_col, alibi_coef(h)))
        qaug_scr[h, :, LANES:] = coef.astype(bf16)

    acc_scr[...] = jnp.zeros(acc_scr.shape, f32)
    ones = jnp.ones((ONES_ROWS, tq), bf16)

    def acc_body(kt, carry):
        for h in range(A_HEADS):
            pm_scr[h] = jnp.exp(logits(kt, h)).astype(bf16)
        for h in range(A_HEADS):
            rows = slice(h * (A_HEAD_DIM + ONES_ROWS), (h + 1) * (A_HEAD_DIM + ONES_ROWS))
            v_ones = jnp.concatenate([vt_ref[kt, h * A_HEAD_DIM:(h + 1) * A_HEAD_DIM, :], ones], axis=0)
            acc_scr[rows, :] += _dot(v_ones, pm_scr[h])
        return carry

    _pair_loop(nkt, acc_body, 0)
    for h in range(A_HEADS):
        r0 = h * (A_HEAD_DIM + ONES_ROWS)
        out_scr[h * A_HEAD_DIM:(h + 1) * A_HEAD_DIM, :] = (acc_scr[r0:r0 + A_HEAD_DIM, :]
                                                           / acc_scr[r0 + A_HEAD_DIM:r0 + A_HEAD_DIM + 1, :])
    o_ref[...] = out_scr[...].T.astype(o_ref.dtype)


def _dsa(qa, qi, wit, klr, ka, vt, *, topk, tq):
    B, Lp, _ = qa.shape
    nq = Lp // tq
    assert nq * (tq // (SUBLANES * (32 // DIGIT_BITS))) < 2 ** (DIGIT_BITS - 1)
    kpos = np.arange(Lp)
    posfeat = np.zeros((Lp, LANES), np.float32)
    posfeat[:, 0] = kpos // tq
    posfeat[:, 1] = kpos % tq
    posfeat[:, 2:4] = 1.0
    posfeat = jnp.asarray(posfeat, bf16)
    qrow = lambda c: pl.BlockSpec((None, tq, c), lambda b, i: (b, i, 0))
    seq = lambda *s: pl.BlockSpec((None,) + s, lambda b, i: (b,) + (0,) * len(s), pipeline_mode=pl.Buffered(1))
    return pl.pallas_call(
        functools.partial(_dsa_kernel, topk=topk, tq=tq),
        grid=(B, nq),
        in_specs=[qrow(A_WIDTH), qrow(IDX_HEADS * IDX_DIM),
                  pl.BlockSpec((None, None, IDX_HEADS, tq), lambda b, i: (b, i, 0, 0)),
                  seq(nq, 2 * tq, LANES), seq(Lp, A_WIDTH), seq(nq, A_WIDTH, tq), _const_spec(posfeat.shape)],
        out_specs=qrow(A_WIDTH),
        out_shape=jax.ShapeDtypeStruct((B, Lp, A_WIDTH), bf16),
        scratch_shapes=[
            pltpu.VMEM((nq, tq, tq), jnp.int32),
            pltpu.VMEM((nq, tq, tq), DIGIT_DTYPE),
            pltpu.VMEM((nq, tq, tq), f32),
            pltpu.VMEM((tq, tq), f32),
            pltpu.VMEM((A_HEADS, tq, 2 * LANES), bf16),
            pltpu.VMEM((A_HEADS, tq, tq), bf16),
            pltpu.VMEM((A_HEADS * (A_HEAD_DIM + ONES_ROWS), tq), f32),
            pltpu.VMEM((A_WIDTH, tq), f32),
        ],
        compiler_params=_params("parallel", "arbitrary"),
        name="dsa",
    )(qa, qi, wit, klr, ka, vt, posfeat)


def _hgrn2_offsets(C):
    base = [HG_SUB * (t // HG_SUB) for t in range(C)]
    ns = [SUBLANES * ((t - base[t]) // SUBLANES + 1) for t in range(C)]
    return base, ns, np.concatenate([[0], np.cumsum(ns)]).tolist()


def _hgrn2_kernel(qb_ref, logf_ref, lk_ref, ib_ref, gb_ref, gain_ref, tril_ref, bones_ref, o_ref,
                  st_scr, a_scr, w_scr, intra_scr):
    C = HG_CHUNK
    P = qb_ref.shape[0]
    pairs = range(P)
    base, ns, off = _hgrn2_offsets(C)

    @pl.when(pl.program_id(1) == 0)
    def _():
        st_scr[...] = jnp.zeros(st_scr.shape, f32)

    tril = tril_ref[...]
    bones = bones_ref[...]

    def cumsum(lf):
        hi = lf.astype(bf16)
        r1 = lf - hi.astype(f32)
        mid = r1.astype(bf16)
        lo = (r1 - mid.astype(f32)).astype(bf16)
        return _dot(tril, hi) + _dot(tril, mid) + _dot(tril, lo)

    b2 = [cumsum(logf_ref[p]) * LOG2E for p in pairs]
    c2 = [b2[p] - lk_ref[p] * LOG2E for p in pairs]
    q = [qb_ref[p] for p in pairs]
    v = [ib_ref[p] for p in pairs]

    inter = [_dot_nt((q[p] * jnp.exp2(b2[p])).astype(bf16), st_scr[p].astype(bf16)) for p in pairs]
    for p in pairs:
        b_last = b2[p][C - 1:C, :]
        k_dec = jnp.exp2(b_last - c2[p]).astype(bf16)
        upd = _dot_tn(v[p].astype(bf16), k_dec)
        st_scr[p] = st_scr[p] * jnp.exp2(b_last) + upd * bones.astype(f32)

    left = lax.broadcasted_iota(jnp.int32, (1, LANES), 1) < B_KEY_DIM
    srow = lax.broadcasted_iota(jnp.int32, (C, 1), 0)
    blocks = range(1, C // HG_SUB)
    scores = {}
    for p in pairs:
        for blk in blocks:
            t0 = blk * HG_SUB
            b_r = b2[p][t0 - 1:t0, :]
            q_dec = q[p][t0:t0 + HG_SUB] * jnp.exp2(b2[p][t0:t0 + HG_SUB] - b_r)
            k_dec = jnp.exp2(jnp.where(srow < t0, b_r - c2[p], -jnp.inf)).astype(bf16)
            lhs = jnp.concatenate([jnp.where(left, q_dec, 0.0), jnp.where(left, 0.0, q_dec)], axis=0).astype(bf16)
            scores[p, blk] = _dot_nt(lhs, k_dec).astype(bf16)
    early = []
    for p in pairs:
        vb = v[p].astype(bf16)
        parts = [jnp.zeros((HG_SUB, LANES), f32)]
        for blk in blocks:
            res = _dot(scores[p, blk], vb)
            parts.append(jnp.where(left, res[:HG_SUB], res[HG_SUB:]))
        early.append(jnp.concatenate(parts, axis=0))

    sidx = lax.broadcasted_iota(jnp.int32, (SUBLANES, 1), 0)
    for p in pairs:
        for t in range(C):
            s0, n = base[t], ns[t]
            diff = b2[p][t:t + 1, :] - c2[p][s0:s0 + n]
            last = jnp.where(sidx <= t % SUBLANES, diff[n - SUBLANES:], -jnp.inf)
            diff = last if n == SUBLANES else jnp.concatenate([diff[:n - SUBLANES], last], axis=0)
            a_scr[p, off[t]:off[t] + n, :] = q[p][t:t + 1, :] * jnp.exp2(diff)
    for p in pairs:
        w_scr[p] = _dot(a_scr[p].astype(bf16), bones)
    for p in pairs:
        for t in range(C):
            s0, n = base[t], ns[t]
            intra_scr[p, t:t + 1, :] = jnp.sum(w_scr[p, off[t]:off[t] + n, :] * v[p][s0:s0 + n], axis=0,
                                               keepdims=True)

    o = [inter[p] + early[p] + intra_scr[p] for p in pairs]
    ss = []
    for p in pairs:
        o2 = o[p] * o[p]
        o2_hi = o2.astype(bf16)
        o2_lo = (o2 - o2_hi.astype(f32)).astype(bf16)
        ss.append(_dot(o2_hi, bones) + _dot(o2_lo, bones))
    for p in pairs:
        on = o[p] * lax.rsqrt(ss[p] * (1.0 / B_KEY_DIM) + LN_EPS) * gain_ref[p]
        g = gb_ref[p]
        o_ref[p] = (on * (g / (1.0 + jnp.exp(-g)))).astype(o_ref.dtype)


def _hgrn2(qb, logf, logk, ib, gb, gain):
    B, P, Lp, W = qb.shape
    C = HG_CHUNK
    head = np.arange(W) // B_KEY_DIM
    tril = jnp.asarray(np.tril(np.ones((C, C), np.float32)), bf16)
    bones = jnp.asarray((head[:, None] == head[None, :]).astype(np.float32), bf16)
    rows = _hgrn2_offsets(C)[2][-1]
    row = pl.BlockSpec((None, P, C, W), lambda b, c: (b, 0, c, 0))
    return pl.pallas_call(
        _hgrn2_kernel,
        grid=(B, Lp // C),
        in_specs=[row] * 5 + [_const_spec(gain.shape), _const_spec(tril.shape), _const_spec(bones.shape)],
        out_specs=row,
        out_shape=jax.ShapeDtypeStruct((B, P, Lp, W), bf16),
        scratch_shapes=[pltpu.VMEM((P, W, W), f32), pltpu.VMEM((P, rows, W), f32), pltpu.VMEM((P, rows, W), f32),
                        pltpu.VMEM((P, C, W), f32)],
        compiler_params=_params("parallel", "arbitrary"),
        name="hgrn2",
    )(qb, logf, logk, ib, gb, gain, tril, bones)


def _ffn_tail(h, mix, g1_ref, b1_ref, wg_ref, wu_ref, wd_ref, g2_ref, b2_ref, out_ref):
    h1 = _layer_norm(ALPHA * h + mix, g1_ref[...], b1_ref[...])
    hb = h1.astype(bf16)
    g = _dot(hb, wg_ref[...])
    u = _dot(hb, wu_ref[...])
    act = (g / (1.0 + jnp.exp(-g))) * u
    f = _dot(act.astype(bf16), wd_ref[...])
    out_ref[...] = _layer_norm(ALPHA * h1 + f, g2_ref[...], b2_ref[...])


def _post_even_kernel(h_ref, oa_ref, ob_ref, wout_ref, *rest):
    ob = jnp.concatenate([ob_ref[p] for p in range(B_WIDTH // LANES)], axis=1)
    mix = _dot(oa_ref[...], wout_ref[:A_WIDTH, :]) + _dot(ob, wout_ref[A_WIDTH:, :])
    _ffn_tail(h_ref[...], mix, *rest)


def _post_odd_kernel(h_ref, prev_ref, wpool_ref, pscale_ref, *rest, tm):
    i = pl.program_id(1)
    h = h_ref[...]
    prev = jnp.where(i == 0, 0.0, prev_ref[...])
    x = jnp.concatenate([prev, h], axis=0)
    pos = i * tm + lax.broadcasted_iota(jnp.int32, (tm, 1), 0)
    parts = []
    for g, win in enumerate(POOL_WINDOWS):
        sl = slice(g * POOL_GROUP_DIM, (g + 1) * POOL_GROUP_DIM)
        s = x[:, sl]
        span = 1
        while span < win:
            s = s + pltpu.roll(s, span, axis=0)
            span *= 2
        cnt = jnp.minimum(pos + 1, win).astype(f32)
        d = s[POOL_HALO:, :] / cnt - h[:, sl]
        parts.append(_dot(d.astype(bf16), wpool_ref[g]))
    mix = jnp.concatenate(parts, axis=-1) * pscale_ref[...]
    _ffn_tail(h, mix, *rest)


POST_TILE_MAX = 576


def _post_tile(Lp):
    return max(t for t in range(POOL_HALO, POST_TILE_MAX + 1, POOL_HALO) if Lp % t == 0)


def _post(h, mixer_inputs, mixer_specs, kernel, tail_consts, *, tm):
    B, Lp, D = h.shape
    row = pl.BlockSpec((None, tm, D), lambda b, i: (b, i, 0))
    return pl.pallas_call(
        kernel,
        grid=(B, Lp // tm),
        in_specs=[row] + mixer_specs + [_const_spec(c.shape) for c in tail_consts],
        out_specs=row,
        out_shape=jax.ShapeDtypeStruct((B, Lp, D), f32),
        compiler_params=_params("parallel", "parallel"),
        name=getattr(kernel, "__name__", "post_odd"),
    )(h, *mixer_inputs, *tail_consts)


def _post_even(h, oa, ob, w_out, tail_consts, *, tm):
    row = lambda c: pl.BlockSpec((None, tm, c), lambda b, i: (b, i, 0))
    pairs = pl.BlockSpec((None, B_WIDTH // LANES, tm, LANES), lambda b, i: (b, 0, i, 0))
    return _post(h, [oa, ob, w_out], [row(A_WIDTH), pairs, _const_spec(w_out.shape)],
                 _post_even_kernel, tail_consts, tm=tm)


def _post_odd(h, w_pool, pool_scale, tail_consts, *, tm):
    D = h.shape[-1]
    halo_per_tile = tm // POOL_HALO
    prev = pl.BlockSpec((None, POOL_HALO, D), lambda b, i: (b, jnp.maximum(i * halo_per_tile - 1, 0), 0))
    kern = functools.partial(_post_odd_kernel, tm=tm)
    kern.__name__ = "post_odd"
    return _post(h, [h, w_pool, pool_scale], [prev, _const_spec(w_pool.shape), _const_spec(pool_scale.shape)],
                 kern, tail_consts, tm=tm)


_W_IN_SPLITS = np.cumsum([0, Q_RANK, A_WIDTH, A_WIDTH, IDX_DIM, IDX_HEADS, B_WIDTH, B_WIDTH, B_WIDTH, B_WIDTH])


def _w_in_seg(w, j):
    return w[:, _W_IN_SPLITS[j]:_W_IN_SPLITS[j + 1]]


def _pack_w_in(w):
    segs = [_w_in_seg(w, j) for j in (0, 1, 3, 3, 5, 6, 7, 8)]
    return jnp.concatenate(segs, axis=1).astype(bf16)


def kernel(x, meta_tokens, w_in, cq_gain, w_uq, w_uq_idx, kidx_gain, kidx_bias, lb_raw, onorm_gain, w_out,
           w_pool, pool_scale, ln_mix_g, ln_mix_b, w_gate, w_up, w_down, ln_ffn_g, ln_ffn_b):
    B, S, D = x.shape
    L = N_META + S
    Lp = -(-L // SEQ_TILE) * SEQ_TILE
    topk = min(TOPK_MAX, S // 4)
    tm = SEQ_TILE

    meta = jnp.broadcast_to(meta_tokens.astype(x.dtype)[None], (B, N_META, D))
    h = jnp.concatenate([meta, x, jnp.zeros((B, Lp - L, D), x.dtype)], axis=1)

    lower = jnp.cumsum(jax.nn.softmax(lb_raw.astype(f32), axis=0), axis=0)
    lower = lower - lower[:1]

    row = lambda v: v.reshape(1, -1).astype(f32)
    for layer in range(DEPTH):
        j = layer // 2
        tail = [row(ln_mix_g[layer]), row(ln_mix_b[layer]), w_gate[layer].astype(bf16), w_up[layer].astype(bf16),
                w_down[layer].astype(bf16), row(ln_ffn_g[layer]), row(ln_ffn_b[layer])]
        if layer % 2 == 0:
            lb = lower[j]
            qa, ka, vt, qi, klr, wit, qb, logf, logk, ib, gb = _inproj(
                h, _pack_w_in(w_in[j]), _w_in_seg(w_in[j], 2).T.astype(bf16), _w_in_seg(w_in[j], 4).T.astype(bf16),
                row(cq_gain[j]), w_uq[j].astype(bf16), w_uq_idx[j].astype(bf16),
                row(jnp.tile(kidx_gain[j], 2)), row(jnp.tile(kidx_bias[j], 2)),
                row(jnp.log(lb)), row(jnp.log1p(-lb)), tm=tm)
            oa = _dsa(qa, qi, wit, klr, ka, vt, topk=topk, tq=SEQ_TILE)
            ob = _hgrn2(qb, logf, logk, ib, gb, onorm_gain[j].astype(f32).reshape(-1, 1, LANES))
            h = _post_even(h, oa, ob, w_out[j].astype(bf16), tail, tm=_post_tile(Lp))
        else:
            h = _post_odd(h, w_pool[j].astype(bf16), row(pool_scale[j]), tail, tm=_post_tile(Lp))
    return h[:, N_META:L]
```

```python
import functools

import jax
import jax.numpy as jnp
import numpy as np
from jax import lax
from jax.experimental import pallas as pl
from jax.experimental.pallas import tpu as pltpu

D_MODEL = 1024
DEPTH = 4
N_META = 16
A_HEADS = 8
A_HEAD_DIM = 64
A_WIDTH = A_HEADS * A_HEAD_DIM
Q_RANK = 256
IDX_HEADS = 16
IDX_DIM = 64
TOPK_MAX = 256
B_HEADS = 8
B_KEY_DIM = 64
B_WIDTH = B_HEADS * B_KEY_DIM
POOL_WINDOWS = (2, 4, 8, 16)
POOL_GROUP_DIM = D_MODEL // len(POOL_WINDOWS)
D_FF = -(-8 * D_MODEL // (3 * 256)) * 256
ALPHA = (2 * DEPTH) ** 0.25
LN_EPS = 1e-5
LOG2E = 1.4426950408889634

LANES = 128
SUBLANES = 8
MXU_DIM = 256
VMEM_LIMIT = 56 * 1024 * 1024

SEQ_TILE = 256
HG_CHUNK = 128
HG_SUB = 16
POOL_HALO = 16
ONES_ROWS = 2 * SUBLANES
DIGIT_BITS = 16
DIGIT_DTYPE = jnp.int16

_SEG_A = (0, Q_RANK + A_WIDTH)
_SEG_K = (_SEG_A[1], _SEG_A[1] + LANES)
_SEG_B = (_SEG_K[1], _SEG_K[1] + 4 * B_WIDTH)
IN_COLS_PADDED = _SEG_B[1]

INT_MIN = -2 ** 31
NEG_BIG = -1e30

bf16 = jnp.bfloat16
f32 = jnp.float32


def _dot(a, b):
    return jnp.dot(a, b, preferred_element_type=f32)


def _dot_nt(a, b):
    return lax.dot_general(a, b, (((1,), (1,)), ((), ())), preferred_element_type=f32)


def _dot_tn(a, b):
    return lax.dot_general(a, b, (((0,), (0,)), ((), ())), preferred_element_type=f32)


def _layer_norm(x, g, b):
    mu = jnp.mean(x, axis=-1, keepdims=True)
    xc = x - mu
    var = jnp.mean(xc * xc, axis=-1, keepdims=True)
    return xc * lax.rsqrt(var + LN_EPS) * g + b


def _const_spec(shape):
    nd = len(shape)
    return pl.BlockSpec(shape, lambda *_: (0,) * nd, pipeline_mode=pl.Buffered(1))


def _params(*sem):
    return pltpu.CompilerParams(dimension_semantics=sem, vmem_limit_bytes=VMEM_LIMIT)


def _inproj_kernel(h_ref, w_ref, wvt_ref, wwt_ref, cqg_ref, wuq_ref, wuqi_ref, kig_ref, kib_ref,
                   loglb_ref, log1mlb_ref,
                   qa_ref, ka_ref, vt_ref, qi_ref, klr_ref, wit_ref, qb_ref, logf_ref, lk_ref, ib_ref, gb_ref):
    tm = h_ref.shape[0]
    hb = h_ref[...].astype(bf16)

    pa = _dot(hb, w_ref[:, _SEG_A[0]:_SEG_A[1]])
    cq = pa[:, :Q_RANK]
    c = cq * lax.rsqrt(jnp.mean(cq * cq, axis=-1, keepdims=True) + LN_EPS) * cqg_ref[...]
    cb = c.astype(bf16)
    qa_ref[...] = (_dot(cb, wuq_ref[...]) * (A_HEAD_DIM ** -0.5)).astype(bf16)
    qi_ref[...] = _dot(cb, wuqi_ref[...]).astype(bf16)
    ka_ref[...] = pa[:, Q_RANK:].astype(bf16)
    vt_ref[...] = _dot_nt(wvt_ref[...], hb).astype(bf16)
    wit_ref[...] = _dot_nt(wwt_ref[...], hb) * (IDX_HEADS ** -0.5 * IDX_DIM ** -0.5)

    kd = _dot(hb, w_ref[:, _SEG_K[0]:_SEG_K[1]])
    left = lax.broadcasted_iota(jnp.int32, (1, LANES), 1) < IDX_DIM
    mu = jnp.sum(jnp.where(left, kd, 0.0), axis=-1, keepdims=True) * (1.0 / IDX_DIM)
    xc = kd - mu
    var = jnp.sum(jnp.where(left, xc * xc, 0.0), axis=-1, keepdims=True) * (1.0 / IDX_DIM)
    kn = xc * lax.rsqrt(var + LN_EPS) * kig_ref[...] + kib_ref[...]
    klr_ref[:tm, :] = jnp.where(left, kn, 0.0).astype(bf16)
    klr_ref[tm:, :] = jnp.where(left, 0.0, kn).astype(bf16)

    pb = _dot(hb, w_ref[:, _SEG_B[0]:_SEG_B[1]])
    z = pb[:, B_WIDTH:2 * B_WIDTH]
    e = jnp.exp(-jnp.abs(z))
    log_sig = jnp.minimum(z, 0.0) - jnp.log1p(e)
    a = loglb_ref[...]
    b = log1mlb_ref[...] + log_sig
    logf = jnp.maximum(a, b) + jnp.log1p(jnp.exp(-jnp.abs(a - b)))
    logk = log1mlb_ref[...] + (log_sig - z)
    for p in range(B_WIDTH // LANES):
        sl = slice(p * LANES, (p + 1) * LANES)
        qb_ref[p] = pb[:, sl]
        logf_ref[p] = logf[:, sl]
        lk_ref[p] = logk[:, sl]
        ib_ref[p] = pb[:, 2 * B_WIDTH + p * LANES:2 * B_WIDTH + (p + 1) * LANES]
        gb_ref[p] = pb[:, 3 * B_WIDTH + p * LANES:3 * B_WIDTH + (p + 1) * LANES]


def _inproj(h, w_cat, w_vt, w_wt, cq_gain, w_uq, w_uq_idx, kig2, kib2, loglb, log1mlb, *, tm):
    B, Lp, D = h.shape
    nt = Lp // tm
    row = lambda c: pl.BlockSpec((None, tm, c), lambda b, i: (b, i, 0))
    tile = lambda r, c: pl.BlockSpec((None, None, r, c), lambda b, i: (b, i, 0, 0))
    tok = lambda c, dt: (row(c), jax.ShapeDtypeStruct((B, Lp, c), dt))
    til = lambda r, c, dt: (tile(r, c), jax.ShapeDtypeStruct((B, nt, r, c), dt))
    npair = B_WIDTH // LANES
    pair = (pl.BlockSpec((None, npair, tm, LANES), lambda b, i: (b, 0, i, 0)),
            jax.ShapeDtypeStruct((B, npair, Lp, LANES), f32))
    outs = [tok(A_WIDTH, bf16), tok(A_WIDTH, bf16), til(A_WIDTH, tm, bf16), tok(IDX_HEADS * IDX_DIM, bf16),
            til(2 * tm, LANES, bf16), til(IDX_HEADS, tm, f32),
            pair, pair, pair, pair, pair]
    consts = [w_cat, w_vt, w_wt, cq_gain, w_uq, w_uq_idx, kig2, kib2, loglb, log1mlb]
    return pl.pallas_call(
        _inproj_kernel,
        grid=(B, nt),
        in_specs=[row(D)] + [_const_spec(c.shape) for c in consts],
        out_specs=[s for s, _ in outs],
        out_shape=[o for _, o in outs],
        compiler_params=_params("parallel", "parallel"),
        name="inproj",
    )(h, *consts)


def _tree(op, xs):
    xs = list(xs)
    while len(xs) > 1:
        xs = [op(xs[j], xs[j + 1]) for j in range(0, len(xs) - 1, 2)] + ([xs[-1]] if len(xs) % 2 else [])
    return xs[0]


def _f32_to_key(x):
    bits = pltpu.bitcast(x, jnp.int32)
    return bits ^ ((bits >> 31) & jnp.int32(0x7FFFFFFF))


TILE_UNROLLS = (4, 2, 1)


def _pair_loop(n, body, carry):
    start = 0
    for unroll in TILE_UNROLLS:
        def trip(j, c, start=start, unroll=unroll):
            for u in range(unroll):
                c = body(start + unroll * j + u, c)
            return c

        trips = (n - start) // unroll
        carry = lax.fori_loop(0, trips, trip, carry)
        start = start + unroll * trips
    return carry


def _dsa_kernel(qa_ref, qi_ref, wit_ref, klr_ref, ka_ref, vt_ref, pos_ref, o_ref,
                keys_scr, digit_scr, bias_scr, sacc_scr, qaug_scr, pm_scr, acc_scr, out_scr, *, topk, tq, seq_len):
    i = pl.program_id(1)
    nkt = i + 1
    lane = lax.broadcasted_iota(jnp.int32, (1, LANES), 1)
    left = lane < A_HEAD_DIM
    q_pos = i * tq + lax.broadcasted_iota(jnp.int32, (1, tq), 1)
    k_iota = lax.broadcasted_iota(jnp.int32, (tq, 1), 0)

    def alibi_coef(h):
        slope = 2.0 ** -(h + 1)
        return jnp.where(lane == 0, slope * tq, jnp.where(lane == 1, slope, 0.0))

    for h in range(A_HEADS):
        qp = qa_ref[:, (h // 2) * LANES:(h // 2 + 1) * LANES]
        keep = left if h % 2 == 0 else jnp.logical_not(left)
        qaug_scr[h, :, :LANES] = jnp.where(keep, qp, jnp.zeros_like(qp))
        qaug_scr[h, :, LANES:] = jnp.broadcast_to(alibi_coef(h).astype(bf16), (tq, LANES))

    def score_body(kt, carry):
        klr = klr_ref[kt]
        for p in range(IDX_HEADS // 2):
            s = _dot_nt(klr, qi_ref[:, p * LANES:(p + 1) * LANES])
            t = (jnp.maximum(s[:tq], 0.0) * wit_ref[2 * p:2 * p + 1, :]
                 + jnp.maximum(s[tq:], 0.0) * wit_ref[2 * p + 1:2 * p + 2, :])
            if p == 0:
                sacc_scr[...] = t
            else:
                sacc_scr[...] += t
        causal = (kt * tq + k_iota) <= q_pos
        keys_scr[kt] = jnp.where(causal, _f32_to_key(sacc_scr[...]), jnp.int32(INT_MIN))
        return carry

    _pair_loop(nkt, score_body, 0)

    w = DIGIT_BITS
    pack = SUBLANES * (32 // w)
    half = 2 ** (w - 1)

    def kth_largest_digit(n_floor):
        def bit_body(it, carry):
            c, n_c = carry
            trial = c + jnp.left_shift(jnp.int32(1), w - 1 - it)
            trial_b = jnp.broadcast_to(trial, (pack, tq)).astype(DIGIT_DTYPE)

            def cnt_body(kt, a):
                return a + _tree(jnp.add, ((digit_scr[kt, r * pack:(r + 1) * pack, :] >= trial_b).astype(DIGIT_DTYPE)
                                           for r in range(tq // pack)))

            a = _pair_loop(nkt, cnt_body, jnp.zeros((pack, tq), DIGIT_DTYPE))
            n = jnp.sum(a.astype(jnp.int32), axis=0, keepdims=True)
            return jnp.where(n >= topk, trial, c), jnp.where(n >= topk, n, n_c)

        return lax.fori_loop(0, w, bit_body, (jnp.full((1, tq), -half, jnp.int32), n_floor))

    prefix = None
    n_ge = jnp.full((1, tq), 1, jnp.int32) * (nkt * tq)
    for stage in range(32 // w):
        shift = 32 - (stage + 1) * w

        def digit_body(kt, carry):
            if prefix is None:
                val = keys_scr[kt] >> shift
            else:
                lo = prefix << (shift + w)
                hi = lo + jnp.int32(2 ** (shift + w) - 1)
                val = ((jnp.clip(keys_scr[kt], lo, hi) - lo) >> shift) - half
            digit_scr[kt] = val.astype(DIGIT_DTYPE)
            return carry

        lax.fori_loop(0, nkt, digit_body, 0)
        digit, n_ge = kth_largest_digit(n_ge)
        prefix = digit if prefix is None else (prefix << w) + (digit + half)
    thr = jnp.maximum(prefix, jnp.int32(INT_MIN + 1))

    excess = jnp.where(jnp.logical_and(prefix > jnp.int32(INT_MIN), q_pos < seq_len), n_ge - topk, 0)
    has_ties = jnp.max(excess) > 0
    pos_bits = (keys_scr.shape[0] * tq - 1).bit_length()

    @pl.when(jnp.logical_not(has_ties))
    def _():
        def bias_body(kt, carry):
            bias_scr[kt] = jnp.where(keys_scr[kt] >= thr, 0.0, NEG_BIG)
            return carry

        lax.fori_loop(0, nkt, bias_body, 0)

    @pl.when(has_ties)
    def _():
        def tied_above(cut):
            def cnt_body(kt, a):
                tied = jnp.logical_and(keys_scr[kt] == thr, (kt * tq + k_iota) > cut)
                return a + _tree(jnp.add, (x for x in (tied.astype(jnp.int32)[r * SUBLANES:(r + 1) * SUBLANES]
                                                       for r in range(tq // SUBLANES))))

            a = lax.fori_loop(0, nkt, cnt_body, jnp.zeros((SUBLANES, tq), jnp.int32))
            return jnp.sum(a, axis=0, keepdims=True)

        def cut_body(it, cut):
            trial = cut | jnp.left_shift(jnp.int32(1), pos_bits - 1 - it)
            return jnp.where(tied_above(trial) >= excess, trial, cut)

        cut = lax.fori_loop(0, pos_bits, cut_body, jnp.zeros((1, tq), jnp.int32))

        def bias_body(kt, carry):
            key = keys_scr[kt]
            keep_tied = jnp.where((kt * tq + k_iota) <= cut, 0.0, NEG_BIG)
            bias_scr[kt] = jnp.where(key > thr, 0.0, jnp.where(key == thr, keep_tied, NEG_BIG))
            return carry

        lax.fori_loop(0, nkt, bias_body, 0)


    groups = tq // SUBLANES
    row_groups = lambda x: (x[r * SUBLANES:(r + 1) * SUBLANES] for r in range(groups))

    def logits(kt, h):
        k0 = pl.multiple_of(kt * tq, tq)
        kaug = jnp.concatenate([ka_ref[pl.ds(k0, tq), (h // 2) * LANES:(h // 2 + 1) * LANES],
                                pos_ref[pl.ds(k0, tq), :]], axis=1)
        return _dot_nt(kaug, qaug_scr[h]) + bias_scr[kt]

    def max_body(kt, m8):
        return tuple(jnp.maximum(m8[h], _tree(jnp.maximum, row_groups(logits(kt, h)))) for h in range(A_HEADS))

    m8 = _pair_loop(nkt, max_body, (jnp.full((SUBLANES, tq), NEG_BIG, f32),) * A_HEADS)

    m_cols = jnp.concatenate([jnp.max(m8[h], axis=0, keepdims=True) for h in range(A_HEADS)], axis=0).T
    for h in range(A_HEADS):
        m_col = m_cols[:, h:h + 1]
        m_hi = m_col.astype(bf16).astype(f32)
        coef = jnp.where(lane == 2, -m_hi, jnp.where(lane == 3, m_hi - m_col, alibi_coef(h)))
        qaug_scr[h, :, LANES:] = coef.astype(bf16)

    acc_scr[...] = jnp.zeros(acc_scr.shape, f32)
    ones = jnp.ones((ONES_ROWS, tq), bf16)

    def acc_body(kt, carry):
        for h in range(A_HEADS):
            pm_scr[h] = jnp.exp(logits(kt, h)).astype(bf16)
        for h in range(A_HEADS):
            rows = slice(h * (A_HEAD_DIM + ONES_ROWS), (h + 1) * (A_HEAD_DIM + ONES_ROWS))
            v_ones = jnp.concatenate([vt_ref[kt, h * A_HEAD_DIM:(h + 1) * A_HEAD_DIM, :], ones], axis=0)
            acc_scr[rows, :] += _dot(v_ones, pm_scr[h])
        return carry

    _pair_loop(nkt, acc_body, 0)
    for h in range(A_HEADS):
        r0 = h * (A_HEAD_DIM + ONES_ROWS)
        out_scr[h * A_HEAD_DIM:(h + 1) * A_HEAD_DIM, :] = (acc_scr[r0:r0 + A_HEAD_DIM, :]
                                                           / acc_scr[r0 + A_HEAD_DIM:r0 + A_HEAD_DIM + 1, :])
    o_ref[...] = out_scr[...].T.astype(o_ref.dtype)


def _dsa(qa, qi, wit, klr, ka, vt, *, topk, tq, seq_len):
    B, Lp, _ = qa.shape
    nq = Lp // tq
    assert nq * (tq // (SUBLANES * (32 // DIGIT_BITS))) < 2 ** (DIGIT_BITS - 1)
    kpos = np.arange(Lp)
    posfeat = np.zeros((Lp, LANES), np.float32)
    posfeat[:, 0] = kpos // tq
    posfeat[:, 1] = kpos % tq
    posfeat[:, 2:4] = 1.0
    posfeat = jnp.asarray(posfeat, bf16)
    qrow = lambda c: pl.BlockSpec((None, tq, c), lambda b, i: (b, i, 0))
    seq = lambda *s: pl.BlockSpec((None,) + s, lambda b, i: (b,) + (0,) * len(s), pipeline_mode=pl.Buffered(1))
    return pl.pallas_call(
        functools.partial(_dsa_kernel, topk=topk, tq=tq, seq_len=seq_len),
        grid=(B, nq),
        in_specs=[qrow(A_WIDTH), qrow(IDX_HEADS * IDX_DIM),
                  pl.BlockSpec((None, None, IDX_HEADS, tq), lambda b, i: (b, i, 0, 0)),
                  seq(nq, 2 * tq, LANES), seq(Lp, A_WIDTH), seq(nq, A_WIDTH, tq), _const_spec(posfeat.shape)],
        out_specs=qrow(A_WIDTH),
        out_shape=jax.ShapeDtypeStruct((B, Lp, A_WIDTH), bf16),
        scratch_shapes=[
            pltpu.VMEM((nq, tq, tq), jnp.int32),
            pltpu.VMEM((nq, tq, tq), DIGIT_DTYPE),
            pltpu.VMEM((nq, tq, tq), f32),
            pltpu.VMEM((tq, tq), f32),
            pltpu.VMEM((A_HEADS, tq, 2 * LANES), bf16),
            pltpu.VMEM((A_HEADS, tq, tq), bf16),
            pltpu.VMEM((A_HEADS * (A_HEAD_DIM + ONES_ROWS), tq), f32),
            pltpu.VMEM((A_WIDTH, tq), f32),
        ],
        compiler_params=_params("parallel", "arbitrary"),
        name="dsa",
    )(qa, qi, wit, klr, ka, vt, posfeat)


def _hgrn2_offsets(C):
    base = [HG_SUB * (t // HG_SUB) for t in range(C)]
    ns = [SUBLANES * ((t - base[t]) // SUBLANES + 1) for t in range(C)]
    return base, ns, np.concatenate([[0], np.cumsum(ns)]).tolist()


def _hgrn2_kernel(qb_ref, logf_ref, lk_ref, ib_ref, gb_ref, gain_ref, tril_ref, bones_ref, o_ref,
                  st_scr, a_scr, w_scr, intra_scr):
    C = HG_CHUNK
    P = qb_ref.shape[0]
    pairs = range(P)
    base, ns, off = _hgrn2_offsets(C)

    @pl.when(pl.program_id(1) == 0)
    def _():
        st_scr[...] = jnp.zeros(st_scr.shape, f32)

    tril = tril_ref[...]
    bones = bones_ref[...]

    def cumsum(lf):
        hi = lf.astype(bf16)
        r1 = lf - hi.astype(f32)
        mid = r1.astype(bf16)
        lo = (r1 - mid.astype(f32)).astype(bf16)
        return _dot(tril, hi) + _dot(tril, mid) + _dot(tril, lo)

    b2 = [cumsum(logf_ref[p]) * LOG2E for p in pairs]
    c2 = [b2[p] - lk_ref[p] * LOG2E for p in pairs]
    q = [qb_ref[p] for p in pairs]
    v = [ib_ref[p] for p in pairs]

    inter = [_dot_nt((q[p] * jnp.exp2(b2[p])).astype(bf16), st_scr[p].astype(bf16)) for p in pairs]
    for p in pairs:
        b_last = b2[p][C - 1:C, :]
        k_dec = jnp.exp2(b_last - c2[p]).astype(bf16)
        upd = _dot_tn(v[p].astype(bf16), k_dec)
        st_scr[p] = st_scr[p] * jnp.exp2(b_last) + upd * bones.astype(f32)

    left = lax.broadcasted_iota(jnp.int32, (1, LANES), 1) < B_KEY_DIM
    srow = lax.broadcasted_iota(jnp.int32, (C, 1), 0)
    blocks = range(1, C // HG_SUB)
    scores = {}
    for p in pairs:
        for blk in blocks:
            t0 = blk * HG_SUB
            b_r = b2[p][t0 - 1:t0, :]
            q_dec = q[p][t0:t0 + HG_SUB] * jnp.exp2(b2[p][t0:t0 + HG_SUB] - b_r)
            k_dec = jnp.exp2(jnp.where(srow < t0, b_r - c2[p], -jnp.inf)).astype(bf16)
            lhs = jnp.concatenate([jnp.where(left, q_dec, 0.0), jnp.where(left, 0.0, q_dec)], axis=0).astype(bf16)
            scores[p, blk] = _dot_nt(lhs, k_dec).astype(bf16)
    early = []
    for p in pairs:
        vb = v[p].astype(bf16)
        parts = [jnp.zeros((HG_SUB, LANES), f32)]
        for blk in blocks:
            res = _dot(scores[p, blk], vb)
            parts.append(jnp.where(left, res[:HG_SUB], res[HG_SUB:]))
        early.append(jnp.concatenate(parts, axis=0))

    sidx = lax.broadcasted_iota(jnp.int32, (SUBLANES, 1), 0)
    for p in pairs:
        for t in range(C):
            s0, n = base[t], ns[t]
            diff = b2[p][t:t + 1, :] - c2[p][s0:s0 + n]
            last = jnp.where(sidx <= t % SUBLANES, diff[n - SUBLANES:], -jnp.inf)
            diff = last if n == SUBLANES else jnp.concatenate([diff[:n - SUBLANES], last], axis=0)
            a_scr[p, off[t]:off[t] + n, :] = q[p][t:t + 1, :] * jnp.exp2(diff)
    for p in pairs:
        w_scr[p] = _dot(a_scr[p].astype(bf16), bones)
    for p in pairs:
        for t in range(C):
            s0, n = base[t], ns[t]
            intra_scr[p, t:t + 1, :] = jnp.sum(w_scr[p, off[t]:off[t] + n, :] * v[p][s0:s0 + n], axis=0,
                                               keepdims=True)

    o = [inter[p] + early[p] + intra_scr[p] for p in pairs]
    ss = []
    for p in pairs:
        o2 = o[p] * o[p]
        o2_hi = o2.astype(bf16)
        o2_lo = (o2 - o2_hi.astype(f32)).astype(bf16)
        ss.append(_dot(o2_hi, bones) + _dot(o2_lo, bones))
    for p in pairs:
        on = o[p] * lax.rsqrt(ss[p] * (1.0 / B_KEY_DIM) + LN_EPS) * gain_ref[p]
        g = gb_ref[p]
        o_ref[p] = (on * (g / (1.0 + jnp.exp(-g)))).astype(o_ref.dtype)


def _hgrn2(qb, logf, logk, ib, gb, gain):
    B, P, Lp, W = qb.shape
    C = HG_CHUNK
    head = np.arange(W) // B_KEY_DIM
    tril = jnp.asarray(np.tril(np.ones((C, C), np.float32)), bf16)
    bones = jnp.asarray((head[:, None] == head[None, :]).astype(np.float32), bf16)
    rows = _hgrn2_offsets(C)[2][-1]
    row = pl.BlockSpec((None, P, C, W), lambda b, c: (b, 0, c, 0))
    return pl.pallas_call(
        _hgrn2_kernel,
        grid=(B, Lp // C),
        in_specs=[row] * 5 + [_const_spec(gain.shape), _const_spec(tril.shape), _const_spec(bones.shape)],
        out_specs=row,
        out_shape=jax.ShapeDtypeStruct((B, P, Lp, W), bf16),
        scratch_shapes=[pltpu.VMEM((P, W, W), f32), pltpu.VMEM((P, rows, W), f32), pltpu.VMEM((P, rows, W), f32),
                        pltpu.VMEM((P, C, W), f32)],
        compiler_params=_params("parallel", "arbitrary"),
        name="hgrn2",
    )(qb, logf, logk, ib, gb, gain, tril, bones)


def _ffn_tail(h, mix, g1_ref, b1_ref, wg_ref, wu_ref, wd_ref, g2_ref, b2_ref, out_ref):
    h1 = _layer_norm(ALPHA * h + mix, g1_ref[...], b1_ref[...])
    hb = h1.astype(bf16)
    g = _dot(hb, wg_ref[...])
    u = _dot(hb, wu_ref[...])
    act = (g / (1.0 + jnp.exp(-g))) * u
    f = _dot(act.astype(bf16), wd_ref[...])
    out_ref[...] = _layer_norm(ALPHA * h1 + f, g2_ref[...], b2_ref[...])


def _post_even_kernel(h_ref, oa_ref, ob_ref, wout_ref, *rest):
    ob = jnp.concatenate([ob_ref[p] for p in range(B_WIDTH // LANES)], axis=1)
    mix = _dot(oa_ref[...], wout_ref[:A_WIDTH, :]) + _dot(ob, wout_ref[A_WIDTH:, :])
    _ffn_tail(h_ref[...], mix, *rest)


def _post_odd_kernel(h_ref, prev_ref, wpool_ref, pscale_ref, *rest, tm):
    i = pl.program_id(1)
    h = h_ref[...]
    prev = jnp.where(i == 0, 0.0, prev_ref[...])
    x = jnp.concatenate([prev, h], axis=0)
    pos = i * tm + lax.broadcasted_iota(jnp.int32, (tm, 1), 0)
    parts = []
    for g, win in enumerate(POOL_WINDOWS):
        sl = slice(g * POOL_GROUP_DIM, (g + 1) * POOL_GROUP_DIM)
        s = x[:, sl]
        span = 1
        while span < win:
            s = s + pltpu.roll(s, span, axis=0)
            span *= 2
        cnt = jnp.minimum(pos + 1, win).astype(f32)
        d = s[POOL_HALO:, :] / cnt - h[:, sl]
        parts.append(_dot(d.astype(bf16), wpool_ref[g]))
    mix = jnp.concatenate(parts, axis=-1) * pscale_ref[...]
    _ffn_tail(h, mix, *rest)


POST_TILE_MAX = 576


def _post_tile(Lp):
    return max(t for t in range(POOL_HALO, POST_TILE_MAX + 1, POOL_HALO) if Lp % t == 0)


def _post(h, mixer_inputs, mixer_specs, kernel, tail_consts, *, tm):
    B, Lp, D = h.shape
    row = pl.BlockSpec((None, tm, D), lambda b, i: (b, i, 0))
    return pl.pallas_call(
        kernel,
        grid=(B, Lp // tm),
        in_specs=[row] + mixer_specs + [_const_spec(c.shape) for c in tail_consts],
        out_specs=row,
        out_shape=jax.ShapeDtypeStruct((B, Lp, D), f32),
        compiler_params=_params("parallel", "parallel"),
        name=getattr(kernel, "__name__", "post_odd"),
    )(h, *mixer_inputs, *tail_consts)


def _post_even(h, oa, ob, w_out, tail_consts, *, tm):
    row = lambda c: pl.BlockSpec((None, tm, c), lambda b, i: (b, i, 0))
    pairs = pl.BlockSpec((None, B_WIDTH // LANES, tm, LANES), lambda b, i: (b, 0, i, 0))
    return _post(h, [oa, ob, w_out], [row(A_WIDTH), pairs, _const_spec(w_out.shape)],
                 _post_even_kernel, tail_consts, tm=tm)


def _post_odd(h, w_pool, pool_scale, tail_consts, *, tm):
    D = h.shape[-1]
    halo_per_tile = tm // POOL_HALO
    prev = pl.BlockSpec((None, POOL_HALO, D), lambda b, i: (b, jnp.maximum(i * halo_per_tile - 1, 0), 0))
    kern = functools.partial(_post_odd_kernel, tm=tm)
    kern.__name__ = "post_odd"
    return _post(h, [h, w_pool, pool_scale], [prev, _const_spec(w_pool.shape), _const_spec(pool_scale.shape)],
                 kern, tail_consts, tm=tm)


_W_IN_SPLITS = np.cumsum([0, Q_RANK, A_WIDTH, A_WIDTH, IDX_DIM, IDX_HEADS, B_WIDTH, B_WIDTH, B_WIDTH, B_WIDTH])


def _w_in_seg(w, j):
    return w[:, _W_IN_SPLITS[j]:_W_IN_SPLITS[j + 1]]


def _pack_w_in(w):
    segs = [_w_in_seg(w, j) for j in (0, 1, 3, 3, 5, 6, 7, 8)]
    return jnp.concatenate(segs, axis=1).astype(bf16)


def kernel(x, meta_tokens, w_in, cq_gain, w_uq, w_uq_idx, kidx_gain, kidx_bias, lb_raw, onorm_gain, w_out,
           w_pool, pool_scale, ln_mix_g, ln_mix_b, w_gate, w_up, w_down, ln_ffn_g, ln_ffn_b):
    B, S, D = x.shape
    L = N_META + S
    Lp = -(-L // SEQ_TILE) * SEQ_TILE
    topk = min(TOPK_MAX, S // 4)
    tm = SEQ_TILE

    meta = jnp.broadcast_to(meta_tokens.astype(x.dtype)[None], (B, N_META, D))
    h = jnp.concatenate([meta, x, jnp.zeros((B, Lp - L, D), x.dtype)], axis=1)

    lower = jnp.cumsum(jax.nn.softmax(lb_raw.astype(f32), axis=0), axis=0)
    lower = lower - lower[:1]

    row = lambda v: v.reshape(1, -1).astype(f32)
    for layer in range(DEPTH):
        j = layer // 2
        tail = [row(ln_mix_g[layer]), row(ln_mix_b[layer]), w_gate[layer].astype(bf16), w_up[layer].astype(bf16),
                w_down[layer].astype(bf16), row(ln_ffn_g[layer]), row(ln_ffn_b[layer])]
        if layer % 2 == 0:
            lb = lower[j]
            qa, ka, vt, qi, klr, wit, qb, logf, logk, ib, gb = _inproj(
                h, _pack_w_in(w_in[j]), _w_in_seg(w_in[j], 2).T.astype(bf16), _w_in_seg(w_in[j], 4).T.astype(bf16),
                row(cq_gain[j]), w_uq[j].astype(bf16), w_uq_idx[j].astype(bf16),
                row(jnp.tile(kidx_gain[j], 2)), row(jnp.tile(kidx_bias[j], 2)),
                row(jnp.log(lb)), row(jnp.log1p(-lb)), tm=tm)
            oa = _dsa(qa, qi, wit, klr, ka, vt, topk=topk, tq=SEQ_TILE, seq_len=L)
            ob = _hgrn2(qb, logf, logk, ib, gb, onorm_gain[j].astype(f32).reshape(-1, 1, LANES))
            h = _post_even(h, oa, ob, w_out[j].astype(bf16), tail, tm=_post_tile(Lp))
        else:
            h = _post_odd(h, w_pool[j].astype(bf16), row(pool_scale[j]), tail, tm=_post_tile(Lp))
    return h[:, N_META:L]
```

```python
import functools

import jax
import jax.numpy as jnp
import numpy as np
from jax import lax
from jax.experimental import pallas as pl
from jax.experimental.pallas import tpu as pltpu

D_MODEL = 1024
DEPTH = 4
N_META = 16
A_HEADS = 8
A_HEAD_DIM = 64
A_WIDTH = A_HEADS * A_HEAD_DIM
Q_RANK = 256
IDX_HEADS = 16
IDX_DIM = 64
TOPK_MAX = 256
B_HEADS = 8
B_KEY_DIM = 64
B_WIDTH = B_HEADS * B_KEY_DIM
POOL_WINDOWS = (2, 4, 8, 16)
POOL_GROUP_DIM = D_MODEL // len(POOL_WINDOWS)
D_FF = -(-8 * D_MODEL // (3 * 256)) * 256
ALPHA = (2 * DEPTH) ** 0.25
LN_EPS = 1e-5
LOG2E = 1.4426950408889634

LANES = 128
SUBLANES = 8
MXU_DIM = 256
VMEM_LIMIT = 56 * 1024 * 1024

SEQ_TILE = 256
HG_CHUNK = 128
HG_SUB = 16
POOL_HALO = 16
ONES_ROWS = 2 * SUBLANES
DIGIT_BITS = 16
DIGIT_DTYPE = jnp.int16

_SEG_A = (0, Q_RANK + A_WIDTH)
_SEG_K = (_SEG_A[1], _SEG_A[1] + LANES)
_SEG_B = (_SEG_K[1], _SEG_K[1] + 4 * B_WIDTH)
IN_COLS_PADDED = _SEG_B[1]

INT_MIN = -2 ** 31
NEG_BIG = -1e30

bf16 = jnp.bfloat16
f32 = jnp.float32


def _dot(a, b):
    return jnp.dot(a, b, preferred_element_type=f32)


def _dot_nt(a, b):
    return lax.dot_general(a, b, (((1,), (1,)), ((), ())), preferred_element_type=f32)


def _dot_tn(a, b):
    return lax.dot_general(a, b, (((0,), (0,)), ((), ())), preferred_element_type=f32)


def _layer_norm(x, g, b):
    mu = jnp.mean(x, axis=-1, keepdims=True)
    xc = x - mu
    var = jnp.mean(xc * xc, axis=-1, keepdims=True)
    return xc * lax.rsqrt(var + LN_EPS) * g + b


def _const_spec(shape):
    nd = len(shape)
    return pl.BlockSpec(shape, lambda *_: (0,) * nd, pipeline_mode=pl.Buffered(1))


def _params(*sem):
    return pltpu.CompilerParams(dimension_semantics=sem, vmem_limit_bytes=VMEM_LIMIT)


def _inproj_kernel(h_ref, w_ref, wvt_ref, wwt_ref, cqg_ref, wuq_ref, wuqi_ref, kig_ref, kib_ref,
                   loglb_ref, log1mlb_ref,
                   qa_ref, ka_ref, vt_ref, qi_ref, klr_ref, wit_ref, qb_ref, logf_ref, lk_ref, ib_ref, gb_ref):
    tm = h_ref.shape[0]
    hb = h_ref[...].astype(bf16)

    pa = _dot(hb, w_ref[:, _SEG_A[0]:_SEG_A[1]])
    cq = pa[:, :Q_RANK]
    c = cq * lax.rsqrt(jnp.mean(cq * cq, axis=-1, keepdims=True) + LN_EPS) * cqg_ref[...]
    cb = c.astype(bf16)
    qa_ref[...] = (_dot(cb, wuq_ref[...]) * (A_HEAD_DIM ** -0.5)).astype(bf16)
    qi_ref[...] = _dot(cb, wuqi_ref[...]).astype(bf16)
    ka_ref[...] = pa[:, Q_RANK:].astype(bf16)
    vt_ref[...] = _dot_nt(wvt_ref[...], hb).astype(bf16)
    wit_ref[...] = _dot_nt(wwt_ref[...], hb) * (IDX_HEADS ** -0.5 * IDX_DIM ** -0.5)

    kd = _dot(hb, w_ref[:, _SEG_K[0]:_SEG_K[1]])
    left = lax.broadcasted_iota(jnp.int32, (1, LANES), 1) < IDX_DIM
    mu = jnp.sum(jnp.where(left, kd, 0.0), axis=-1, keepdims=True) * (1.0 / IDX_DIM)
    xc = kd - mu
    var = jnp.sum(jnp.where(left, xc * xc, 0.0), axis=-1, keepdims=True) * (1.0 / IDX_DIM)
    kn = xc * lax.rsqrt(var + LN_EPS) * kig_ref[...] + kib_ref[...]
    klr_ref[:tm, :] = jnp.where(left, kn, 0.0).astype(bf16)
    klr_ref[tm:, :] = jnp.where(left, 0.0, kn).astype(bf16)

    pb = _dot(hb, w_ref[:, _SEG_B[0]:_SEG_B[1]])
    z = pb[:, B_WIDTH:2 * B_WIDTH]
    e = jnp.exp(-jnp.abs(z))
    log_sig = jnp.minimum(z, 0.0) - jnp.log1p(e)
    a = loglb_ref[...]
    b = log1mlb_ref[...] + log_sig
    logf = jnp.maximum(a, b) + jnp.log1p(jnp.exp(-jnp.abs(a - b)))
    logk = log1mlb_ref[...] + (log_sig - z)
    for p in range(B_WIDTH // LANES):
        sl = slice(p * LANES, (p + 1) * LANES)
        qb_ref[p] = pb[:, sl]
        logf_ref[p] = logf[:, sl]
        lk_ref[p] = logk[:, sl]
        ib_ref[p] = pb[:, 2 * B_WIDTH + p * LANES:2 * B_WIDTH + (p + 1) * LANES]
        gb_ref[p] = pb[:, 3 * B_WIDTH + p * LANES:3 * B_WIDTH + (p + 1) * LANES]


def _inproj(h, w_cat, w_vt, w_wt, cq_gain, w_uq, w_uq_idx, kig2, kib2, loglb, log1mlb, *, tm):
    B, Lp, D = h.shape
    nt = Lp // tm
    row = lambda c: pl.BlockSpec((None, tm, c), lambda b, i: (b, i, 0))
    tile = lambda r, c: pl.BlockSpec((None, None, r, c), lambda b, i: (b, i, 0, 0))
    tok = lambda c, dt: (row(c), jax.ShapeDtypeStruct((B, Lp, c), dt))
    til = lambda r, c, dt: (tile(r, c), jax.ShapeDtypeStruct((B, nt, r, c), dt))
    npair = B_WIDTH // LANES
    pair = (pl.BlockSpec((None, npair, tm, LANES), lambda b, i: (b, 0, i, 0)),
            jax.ShapeDtypeStruct((B, npair, Lp, LANES), f32))
    outs = [tok(A_WIDTH, bf16), tok(A_WIDTH, bf16), til(A_WIDTH, tm, bf16), tok(IDX_HEADS * IDX_DIM, bf16),
            til(2 * tm, LANES, bf16), til(IDX_HEADS, tm, f32),
            pair, pair, pair, pair, pair]
    consts = [w_cat, w_vt, w_wt, cq_gain, w_uq, w_uq_idx, kig2, kib2, loglb, log1mlb]
    return pl.pallas_call(
        _inproj_kernel,
        grid=(B, nt),
        in_specs=[row(D)] + [_const_spec(c.shape) for c in consts],
        out_specs=[s for s, _ in outs],
        out_shape=[o for _, o in outs],
        compiler_params=_params("parallel", "parallel"),
        name="inproj",
    )(h, *consts)


def _tree(op, xs):
    xs = list(xs)
    while len(xs) > 1:
        xs = [op(xs[j], xs[j + 1]) for j in range(0, len(xs) - 1, 2)] + ([xs[-1]] if len(xs) % 2 else [])
    return xs[0]


def _f32_to_key(x):
    bits = pltpu.bitcast(x, jnp.int32)
    return bits ^ ((bits >> 31) & jnp.int32(0x7FFFFFFF))


TILE_UNROLLS = (4, 2, 1)


def _key_to_f32(key):
    return pltpu.bitcast(key ^ ((key >> 31) & jnp.int32(0x7FFFFFFF)), f32)


def _pair_loop(n, body, carry):
    start = 0
    for unroll in TILE_UNROLLS:
        def trip(j, c, start=start, unroll=unroll):
            for u in range(unroll):
                c = body(start + unroll * j + u, c)
            return c

        trips = (n - start) // unroll
        carry = lax.fori_loop(0, trips, trip, carry)
        start = start + unroll * trips
    return carry


def _dsa_kernel(qa_ref, qi_ref, wit_ref, klr_ref, ka_ref, vt_ref, pos_ref, o_ref,
                keys_scr, sc_scr, digit_scr, thr_scr, exc_scr, bias_scr, sacc_scr, qaug_scr, pm_scr, acc_scr, out_scr,
                *, topk, tq, seq_len):
    i = pl.program_id(1)
    nkt = i + 1
    lane = lax.broadcasted_iota(jnp.int32, (1, LANES), 1)
    left = lane < A_HEAD_DIM
    q_pos = i * tq + lax.broadcasted_iota(jnp.int32, (1, tq), 1)
    k_iota = lax.broadcasted_iota(jnp.int32, (tq, 1), 0)

    def alibi_coef(h):
        slope = 2.0 ** -(h + 1)
        return jnp.where(lane == 0, slope * tq, jnp.where(lane == 1, slope, 0.0))

    keep = (jnp.where(left, 1.0, 0.0).astype(bf16), jnp.where(left, 0.0, 1.0).astype(bf16))
    for h in range(A_HEADS):
        qaug_scr[h, :, :LANES] = qa_ref[:, (h // 2) * LANES:(h // 2 + 1) * LANES] * keep[h % 2]
        qaug_scr[h, :, LANES:] = jnp.broadcast_to(alibi_coef(h).astype(bf16), (tq, LANES))

    def score_body(kt, carry):
        klr = klr_ref[kt]
        for p in range(IDX_HEADS // 2):
            s = _dot_nt(klr, qi_ref[:, p * LANES:(p + 1) * LANES])
            t = (jnp.maximum(s[:tq], 0.0) * wit_ref[2 * p:2 * p + 1, :]
                 + jnp.maximum(s[tq:], 0.0) * wit_ref[2 * p + 1:2 * p + 2, :])
            if p == 0:
                sacc_scr[...] = t
            else:
                sacc_scr[...] += t
        causal = (kt * tq + k_iota) <= q_pos
        sc_scr[kt] = jnp.where(causal, sacc_scr[...], -jnp.inf)
        keys_scr[kt] = jnp.where(causal, _f32_to_key(sacc_scr[...]), jnp.int32(INT_MIN))
        return carry

    _pair_loop(nkt, score_body, 0)

    w = DIGIT_BITS
    pack = SUBLANES * (32 // w)
    half = 2 ** (w - 1)

    def kth_largest_digit(n_floor):
        def bit_body(it, carry):
            c, n_c = carry
            trial = c + jnp.left_shift(jnp.int32(1), w - 1 - it)
            trial_b = jnp.broadcast_to(trial, (pack, tq)).astype(DIGIT_DTYPE)

            def cnt_body(kt, a):
                return a + _tree(jnp.add, ((digit_scr[kt, r * pack:(r + 1) * pack, :] >= trial_b).astype(DIGIT_DTYPE)
                                           for r in range(tq // pack)))

            a = _pair_loop(nkt, cnt_body, jnp.zeros((pack, tq), DIGIT_DTYPE))
            n = jnp.sum(a.astype(jnp.int32), axis=0, keepdims=True)
            return jnp.where(n >= topk, trial, c), jnp.where(n >= topk, n, n_c)

        return lax.fori_loop(0, w, bit_body, (jnp.full((1, tq), -half, jnp.int32), n_floor))

    prefix = None
    n_ge = jnp.full((1, tq), 1, jnp.int32) * (nkt * tq)
    for stage in range(32 // w):
        shift = 32 - (stage + 1) * w

        def digit_body(kt, carry):
            if prefix is None:
                val = keys_scr[kt] >> shift
            else:
                lo = prefix << (shift + w)
                hi = lo + jnp.int32(2 ** (shift + w) - 1)
                val = ((jnp.clip(keys_scr[kt], lo, hi) - lo) >> shift) - half
            digit_scr[kt] = val.astype(DIGIT_DTYPE)
            return carry

        lax.fori_loop(0, nkt, digit_body, 0)
        digit, n_ge = kth_largest_digit(n_ge)
        prefix = digit if prefix is None else (prefix << w) + (digit + half)
    def count_scores_ge(trial):
        trial_b = jnp.broadcast_to(trial, (SUBLANES, tq))

        def cnt_body(kt, a):
            return a + _tree(jnp.add, ((sc_scr[kt, r * SUBLANES:(r + 1) * SUBLANES, :] >= trial_b).astype(jnp.int32)
                                       for r in range(tq // SUBLANES)))

        return jnp.sum(_pair_loop(nkt, cnt_body, jnp.zeros((SUBLANES, tq), jnp.int32)), axis=0, keepdims=True)

    def set_threshold(key, n_at_key):
        found = key > jnp.int32(INT_MIN)
        thr_scr[...] = jnp.where(found, _key_to_f32(key), jnp.finfo(f32).min)
        exc_scr[...] = jnp.where(jnp.logical_and(found, q_pos < seq_len), n_at_key - topk, 0)
        return found

    found = set_threshold(prefix, n_ge)
    mismatch = jnp.logical_and(found, count_scores_ge(thr_scr[...]) != n_ge)

    @pl.when(jnp.max(mismatch.astype(jnp.int32)) > 0)
    def _():
        def bit_body(it, carry):
            c, n_c = carry
            trial = c + jnp.left_shift(jnp.int32(1), 31 - it)
            n = count_scores_ge(_key_to_f32(trial))
            return jnp.where(n >= topk, trial, c), jnp.where(n >= topk, n, n_c)

        c, n_c = lax.fori_loop(0, 32, bit_body, (jnp.full((1, tq), INT_MIN, jnp.int32),
                                                 jnp.full((1, tq), 1, jnp.int32) * (nkt * tq)))
        set_threshold(c, n_c)

    thr = thr_scr[...]
    excess = exc_scr[...]
    has_ties = jnp.max(excess) > 0
    pos_bits = (keys_scr.shape[0] * tq - 1).bit_length()

    @pl.when(jnp.logical_not(has_ties))
    def _():
        def bias_body(kt, carry):
            bias_scr[kt] = jnp.where(sc_scr[kt] >= thr, 0.0, NEG_BIG)
            return carry

        lax.fori_loop(0, nkt, bias_body, 0)

    @pl.when(has_ties)
    def _():
        def tied_above(cut):
            def cnt_body(kt, a):
                tied = jnp.logical_and(sc_scr[kt] == thr, (kt * tq + k_iota) > cut)
                return a + _tree(jnp.add, (x for x in (tied.astype(jnp.int32)[r * SUBLANES:(r + 1) * SUBLANES]
                                                       for r in range(tq // SUBLANES))))

            a = lax.fori_loop(0, nkt, cnt_body, jnp.zeros((SUBLANES, tq), jnp.int32))
            return jnp.sum(a, axis=0, keepdims=True)

        def cut_body(it, cut):
            trial = cut | jnp.left_shift(jnp.int32(1), pos_bits - 1 - it)
            return jnp.where(tied_above(trial) >= excess, trial, cut)

        cut = lax.fori_loop(0, pos_bits, cut_body, jnp.zeros((1, tq), jnp.int32))

        def bias_body(kt, carry):
            key = sc_scr[kt]
            keep_tied = jnp.where((kt * tq + k_iota) <= cut, 0.0, NEG_BIG)
            bias_scr[kt] = jnp.where(key > thr, 0.0, jnp.where(key == thr, keep_tied, NEG_BIG))
            return carry

        lax.fori_loop(0, nkt, bias_body, 0)


    groups = tq // SUBLANES
    row_groups = lambda x: (x[r * SUBLANES:(r + 1) * SUBLANES] for r in range(groups))

    def logits(kt, h):
        k0 = pl.multiple_of(kt * tq, tq)
        kaug = jnp.concatenate([ka_ref[pl.ds(k0, tq), (h // 2) * LANES:(h // 2 + 1) * LANES],
                                pos_ref[pl.ds(k0, tq), :]], axis=1)
        return _dot_nt(kaug, qaug_scr[h]) + bias_scr[kt]

    def max_body(kt, m8):
        return tuple(jnp.maximum(m8[h], _tree(jnp.maximum, row_groups(logits(kt, h)))) for h in range(A_HEADS))

    m8 = _pair_loop(nkt, max_body, (jnp.full((SUBLANES, tq), NEG_BIG, f32),) * A_HEADS)

    m_cols = jnp.concatenate([jnp.max(m8[h], axis=0, keepdims=True) for h in range(A_HEADS)], axis=0).T
    for h in range(A_HEADS):
        m_col = m_cols[:, h:h + 1]
        m_hi = m_col.astype(bf16).astype(f32)
        coef = jnp.where(lane == 2, -m_hi, jnp.where(lane == 3, m_hi - m_col, alibi_coef(h)))
        qaug_scr[h, :, LANES:] = coef.astype(bf16)

    acc_scr[...] = jnp.zeros(acc_scr.shape, f32)
    ones = jnp.ones((ONES_ROWS, tq), bf16)

    def acc_body(kt, carry):
        for h in range(A_HEADS):
            pm_scr[h] = jnp.exp(logits(kt, h)).astype(bf16)
        for h in range(A_HEADS):
            rows = slice(h * (A_HEAD_DIM + ONES_ROWS), (h + 1) * (A_HEAD_DIM + ONES_ROWS))
            v_ones = jnp.concatenate([vt_ref[kt, h * A_HEAD_DIM:(h + 1) * A_HEAD_DIM, :], ones], axis=0)
            acc_scr[rows, :] += _dot(v_ones, pm_scr[h])
        return carry

    _pair_loop(nkt, acc_body, 0)
    for h in range(A_HEADS):
        r0 = h * (A_HEAD_DIM + ONES_ROWS)
        out_scr[h * A_HEAD_DIM:(h + 1) * A_HEAD_DIM, :] = (acc_scr[r0:r0 + A_HEAD_DIM, :]
                                                           / acc_scr[r0 + A_HEAD_DIM:r0 + A_HEAD_DIM + 1, :])
    o_ref[...] = out_scr[...].T.astype(o_ref.dtype)


def _dsa(qa, qi, wit, klr, ka, vt, *, topk, tq, seq_len):
    B, Lp, _ = qa.shape
    nq = Lp // tq
    assert nq * (tq // (SUBLANES * (32 // DIGIT_BITS))) < 2 ** (DIGIT_BITS - 1)
    kpos = np.arange(Lp)
    posfeat = np.zeros((Lp, LANES), np.float32)
    posfeat[:, 0] = kpos // tq
    posfeat[:, 1] = kpos % tq
    posfeat[:, 2:4] = 1.0
    posfeat = jnp.asarray(posfeat, bf16)
    qrow = lambda c: pl.BlockSpec((None, tq, c), lambda b, i: (b, i, 0))
    seq = lambda *s: pl.BlockSpec((None,) + s, lambda b, i: (b,) + (0,) * len(s))
    return pl.pallas_call(
        functools.partial(_dsa_kernel, topk=topk, tq=tq, seq_len=seq_len),
        grid=(B, nq),
        in_specs=[qrow(A_WIDTH), qrow(IDX_HEADS * IDX_DIM),
                  pl.BlockSpec((None, None, IDX_HEADS, tq), lambda b, i: (b, i, 0, 0)),
                  seq(nq, 2 * tq, LANES), seq(Lp, A_WIDTH), seq(nq, A_WIDTH, tq), _const_spec(posfeat.shape)],
        out_specs=qrow(A_WIDTH),
        out_shape=jax.ShapeDtypeStruct((B, Lp, A_WIDTH), bf16),
        scratch_shapes=[
            pltpu.VMEM((nq, tq, tq), jnp.int32),
            pltpu.VMEM((nq, tq, tq), f32),
            pltpu.VMEM((nq, tq, tq), DIGIT_DTYPE),
            pltpu.VMEM((1, tq), f32),
            pltpu.VMEM((1, tq), jnp.int32),
            pltpu.VMEM((nq, tq, tq), f32),
            pltpu.VMEM((tq, tq), f32),
            pltpu.VMEM((A_HEADS, tq, 2 * LANES), bf16),
            pltpu.VMEM((A_HEADS, tq, tq), bf16),
            pltpu.VMEM((A_HEADS * (A_HEAD_DIM + ONES_ROWS), tq), f32),
            pltpu.VMEM((A_WIDTH, tq), f32),
        ],
        compiler_params=_params("parallel", "arbitrary"),
        name="dsa",
    )(qa, qi, wit, klr, ka, vt, posfeat)


def _hgrn2_offsets(C):
    base = [HG_SUB * (t // HG_SUB) for t in range(C)]
    ns = [SUBLANES * ((t - base[t]) // SUBLANES + 1) for t in range(C)]
    return base, ns, np.concatenate([[0], np.cumsum(ns)]).tolist()


def _hgrn2_kernel(qb_ref, logf_ref, lk_ref, ib_ref, gb_ref, gain_ref, tril_ref, bones_ref, o_ref,
                  st_scr, a_scr, w_scr, intra_scr):
    C = HG_CHUNK
    P = qb_ref.shape[0]
    pairs = range(P)
    base, ns, off = _hgrn2_offsets(C)

    @pl.when(pl.program_id(1) == 0)
    def _():
        st_scr[...] = jnp.zeros(st_scr.shape, f32)

    tril = tril_ref[...]
    bones = bones_ref[...]

    def cumsum(lf):
        hi = lf.astype(bf16)
        r1 = lf - hi.astype(f32)
        mid = r1.astype(bf16)
        lo = (r1 - mid.astype(f32)).astype(bf16)
        return _dot(tril, hi) + _dot(tril, mid) + _dot(tril, lo)

    b2 = [cumsum(logf_ref[p]) * LOG2E for p in pairs]
    c2 = [b2[p] - lk_ref[p] * LOG2E for p in pairs]
    q = [qb_ref[p] for p in pairs]
    v = [ib_ref[p] for p in pairs]

    inter = [_dot_nt((q[p] * jnp.exp2(b2[p])).astype(bf16), st_scr[p].astype(bf16)) for p in pairs]
    for p in pairs:
        b_last = b2[p][C - 1:C, :]
        k_dec = jnp.exp2(b_last - c2[p]).astype(bf16)
        upd = _dot_tn(v[p].astype(bf16), k_dec)
        st_scr[p] = st_scr[p] * jnp.exp2(b_last) + upd * bones.astype(f32)

    left = lax.broadcasted_iota(jnp.int32, (1, LANES), 1) < B_KEY_DIM
    srow = lax.broadcasted_iota(jnp.int32, (C, 1), 0)
    blocks = range(1, C // HG_SUB)
    scores = {}
    for p in pairs:
        for blk in blocks:
            t0 = blk * HG_SUB
            b_r = b2[p][t0 - 1:t0, :]
            q_dec = q[p][t0:t0 + HG_SUB] * jnp.exp2(b2[p][t0:t0 + HG_SUB] - b_r)
            k_dec = jnp.exp2(jnp.where(srow < t0, b_r - c2[p], -jnp.inf)).astype(bf16)
            lhs = jnp.concatenate([jnp.where(left, q_dec, 0.0), jnp.where(left, 0.0, q_dec)], axis=0).astype(bf16)
            scores[p, blk] = _dot_nt(lhs, k_dec).astype(bf16)
    early = []
    for p in pairs:
        vb = v[p].astype(bf16)
        parts = [jnp.zeros((HG_SUB, LANES), f32)]
        for blk in blocks:
            res = _dot(scores[p, blk], vb)
            parts.append(jnp.where(left, res[:HG_SUB], res[HG_SUB:]))
        early.append(jnp.concatenate(parts, axis=0))

    sidx = lax.broadcasted_iota(jnp.int32, (SUBLANES, 1), 0)
    for p in pairs:
        for t in range(C):
            s0, n = base[t], ns[t]
            diff = b2[p][t:t + 1, :] - c2[p][s0:s0 + n]
            last = jnp.where(sidx <= t % SUBLANES, diff[n - SUBLANES:], -jnp.inf)
            diff = last if n == SUBLANES else jnp.concatenate([diff[:n - SUBLANES], last], axis=0)
            a_scr[p, off[t]:off[t] + n, :] = q[p][t:t + 1, :] * jnp.exp2(diff)
    for p in pairs:
        w_scr[p] = _dot(a_scr[p].astype(bf16), bones)
    for p in pairs:
        for t in range(C):
            s0, n = base[t], ns[t]
            intra_scr[p, t:t + 1, :] = jnp.sum(w_scr[p, off[t]:off[t] + n, :] * v[p][s0:s0 + n], axis=0,
                                               keepdims=True)

    o = [inter[p] + early[p] + intra_scr[p] for p in pairs]
    ss = []
    for p in pairs:
        o2 = o[p] * o[p]
        o2_hi = o2.astype(bf16)
        o2_lo = (o2 - o2_hi.astype(f32)).astype(bf16)
        ss.append(_dot(o2_hi, bones) + _dot(o2_lo, bones))
    for p in pairs:
        on = o[p] * lax.rsqrt(ss[p] * (1.0 / B_KEY_DIM) + LN_EPS) * gain_ref[p]
        g = gb_ref[p]
        o_ref[p] = (on * (g / (1.0 + jnp.exp(-g)))).astype(o_ref.dtype)


def _hgrn2(qb, logf, logk, ib, gb, gain):
    B, P, Lp, W = qb.shape
    C = HG_CHUNK
    head = np.arange(W) // B_KEY_DIM
    tril = jnp.asarray(np.tril(np.ones((C, C), np.float32)), bf16)
    bones = jnp.asarray((head[:, None] == head[None, :]).astype(np.float32), bf16)
    rows = _hgrn2_offsets(C)[2][-1]
    row = pl.BlockSpec((None, P, C, W), lambda b, c: (b, 0, c, 0))
    return pl.pallas_call(
        _hgrn2_kernel,
        grid=(B, Lp // C),
        in_specs=[row] * 5 + [_const_spec(gain.shape), _const_spec(tril.shape), _const_spec(bones.shape)],
        out_specs=row,
        out_shape=jax.ShapeDtypeStruct((B, P, Lp, W), bf16),
        scratch_shapes=[pltpu.VMEM((P, W, W), f32), pltpu.VMEM((P, rows, W), f32), pltpu.VMEM((P, rows, W), f32),
                        pltpu.VMEM((P, C, W), f32)],
        compiler_params=_params("parallel", "arbitrary"),
        name="hgrn2",
    )(qb, logf, logk, ib, gb, gain, tril, bones)


def _ffn_tail(h, mix, g1_ref, b1_ref, wg_ref, wu_ref, wd_ref, g2_ref, b2_ref, out_ref):
    h1 = _layer_norm(ALPHA * h + mix, g1_ref[...], b1_ref[...])
    hb = h1.astype(bf16)
    g = _dot(hb, wg_ref[...])
    u = _dot(hb, wu_ref[...])
    act = (g / (1.0 + jnp.exp(-g))) * u
    f = _dot(act.astype(bf16), wd_ref[...])
    out_ref[...] = _layer_norm(ALPHA * h1 + f, g2_ref[...], b2_ref[...])


def _post_even_kernel(h_ref, oa_ref, ob_ref, wout_ref, *rest):
    ob = jnp.concatenate([ob_ref[p] for p in range(B_WIDTH // LANES)], axis=1)
    mix = _dot(oa_ref[...], wout_ref[:A_WIDTH, :]) + _dot(ob, wout_ref[A_WIDTH:, :])
    _ffn_tail(h_ref[...], mix, *rest)


def _post_odd_kernel(h_ref, prev_ref, wpool_ref, pscale_ref, *rest, tm, skip):
    row0 = skip + pl.program_id(1) * tm
    h = h_ref[...].reshape(h_ref.shape[-2:])
    prev = jnp.where(row0 == 0, 0.0, prev_ref[...])
    x = jnp.concatenate([prev, h], axis=0)
    pos = row0 + lax.broadcasted_iota(jnp.int32, (tm, 1), 0)
    parts = []
    for g, win in enumerate(POOL_WINDOWS):
        sl = slice(g * POOL_GROUP_DIM, (g + 1) * POOL_GROUP_DIM)
        s = x[:, sl]
        span = 1
        while span < win:
            s = s + pltpu.roll(s, span, axis=0)
            span *= 2
        cnt = jnp.minimum(pos + 1, win).astype(f32)
        d = s[POOL_HALO:, :] / cnt - h[:, sl]
        parts.append(_dot(d.astype(bf16), wpool_ref[g]))
    mix = jnp.concatenate(parts, axis=-1) * pscale_ref[...]
    _ffn_tail(h, mix, *rest)


POST_TILE_MAX = 576


def _post_tile(Lp):
    return max(t for t in range(POOL_HALO, POST_TILE_MAX + 1, POOL_HALO) if Lp % t == 0)


def _post(h, mixer_inputs, mixer_specs, kernel, tail_consts, *, tm, skip=0, out_len=None):
    B, Lp, D = h.shape
    out_len = Lp if out_len is None else out_len
    row = pl.BlockSpec((None, tm, D), lambda b, i: (b, i, 0))
    h_rows = row if skip == 0 else pl.BlockSpec((pl.Element(1), pl.Element(tm), pl.Element(D)),
                                                lambda b, i: (b, pl.multiple_of(skip + i * tm, POOL_HALO), 0))
    return pl.pallas_call(
        kernel,
        grid=(B, out_len // tm),
        in_specs=[h_rows] + mixer_specs + [_const_spec(c.shape) for c in tail_consts],
        out_specs=row,
        out_shape=jax.ShapeDtypeStruct((B, out_len, D), f32),
        compiler_params=_params("parallel", "parallel"),
        name=getattr(kernel, "__name__", "post_odd"),
    )(h, *mixer_inputs, *tail_consts)


def _post_even(h, oa, ob, w_out, tail_consts, *, tm):
    row = lambda c: pl.BlockSpec((None, tm, c), lambda b, i: (b, i, 0))
    pairs = pl.BlockSpec((None, B_WIDTH // LANES, tm, LANES), lambda b, i: (b, 0, i, 0))
    return _post(h, [oa, ob, w_out], [row(A_WIDTH), pairs, _const_spec(w_out.shape)],
                 _post_even_kernel, tail_consts, tm=tm)


def _post_odd(h, w_pool, pool_scale, tail_consts, *, tm, skip=0, out_len=None):
    D = h.shape[-1]
    assert tm % POOL_HALO == 0 and skip % POOL_HALO == 0
    prev = pl.BlockSpec((None, POOL_HALO, D),
                        lambda b, i: (b, jnp.maximum((skip + i * tm) // POOL_HALO - 1, 0), 0))
    kern = functools.partial(_post_odd_kernel, tm=tm, skip=skip)
    kern.__name__ = "post_odd"
    return _post(h, [h, w_pool, pool_scale], [prev, _const_spec(w_pool.shape), _const_spec(pool_scale.shape)],
                 kern, tail_consts, tm=tm, skip=skip, out_len=out_len)


_W_IN_SPLITS = np.cumsum([0, Q_RANK, A_WIDTH, A_WIDTH, IDX_DIM, IDX_HEADS, B_WIDTH, B_WIDTH, B_WIDTH, B_WIDTH])


def _w_in_seg(w, j):
    return w[:, _W_IN_SPLITS[j]:_W_IN_SPLITS[j + 1]]


def _pack_w_in(w):
    segs = [_w_in_seg(w, j) for j in (0, 1, 3, 3, 5, 6, 7, 8)]
    return jnp.concatenate(segs, axis=1).astype(bf16)


def kernel(x, meta_tokens, w_in, cq_gain, w_uq, w_uq_idx, kidx_gain, kidx_bias, lb_raw, onorm_gain, w_out,
           w_pool, pool_scale, ln_mix_g, ln_mix_b, w_gate, w_up, w_down, ln_ffn_g, ln_ffn_b):
    B, S, D = x.shape
    L = N_META + S
    Lp = -(-L // SEQ_TILE) * SEQ_TILE
    topk = min(TOPK_MAX, S // 4)
    tm = SEQ_TILE

    meta = jnp.broadcast_to(meta_tokens.astype(x.dtype)[None], (B, N_META, D))
    h = jnp.concatenate([meta, x, jnp.zeros((B, Lp - L, D), x.dtype)], axis=1)

    lower = jnp.cumsum(jax.nn.softmax(lb_raw.astype(f32), axis=0), axis=0)
    lower = lower - lower[:1]

    row = lambda v: v.reshape(1, -1).astype(f32)
    for layer in range(DEPTH):
        j = layer // 2
        tail = [row(ln_mix_g[layer]), row(ln_mix_b[layer]), w_gate[layer].astype(bf16), w_up[layer].astype(bf16),
                w_down[layer].astype(bf16), row(ln_ffn_g[layer]), row(ln_ffn_b[layer])]
        if layer % 2 == 0:
            lb = lower[j]
            qa, ka, vt, qi, klr, wit, qb, logf, logk, ib, gb = _inproj(
                h, _pack_w_in(w_in[j]), _w_in_seg(w_in[j], 2).T.astype(bf16), _w_in_seg(w_in[j], 4).T.astype(bf16),
                row(cq_gain[j]), w_uq[j].astype(bf16), w_uq_idx[j].astype(bf16),
                row(jnp.tile(kidx_gain[j], 2)), row(jnp.tile(kidx_bias[j], 2)),
                row(jnp.log(lb)), row(jnp.log1p(-lb)), tm=tm)
            oa = _dsa(qa, qi, wit, klr, ka, vt, topk=topk, tq=SEQ_TILE, seq_len=L)
            ob = _hgrn2(qb, logf, logk, ib, gb, onorm_gain[j].astype(f32).reshape(-1, 1, LANES))
            h = _post_even(h, oa, ob, w_out[j].astype(bf16), tail, tm=_post_tile(Lp))
        elif layer < DEPTH - 1:
            h = _post_odd(h, w_pool[j].astype(bf16), row(pool_scale[j]), tail, tm=_post_tile(Lp))
        else:
            h = _post_odd(h, w_pool[j].astype(bf16), row(pool_scale[j]), tail, tm=_post_tile(S), skip=N_META, out_len=S)
    return h
```

```python
import functools

import jax
import jax.numpy as jnp
import numpy as np
from jax import lax
from jax.experimental import pallas as pl
from jax.experimental.pallas import tpu as pltpu

D_MODEL = 1024
DEPTH = 4
N_META = 16
A_HEADS = 8
A_HEAD_DIM = 64
A_WIDTH = A_HEADS * A_HEAD_DIM
Q_RANK = 256
IDX_HEADS = 16
IDX_DIM = 64
TOPK_MAX = 256
B_HEADS = 8
B_KEY_DIM = 64
B_WIDTH = B_HEADS * B_KEY_DIM
POOL_WINDOWS = (2, 4, 8, 16)
POOL_GROUP_DIM = D_MODEL // len(POOL_WINDOWS)
D_FF = -(-8 * D_MODEL // (3 * 256)) * 256
ALPHA = (2 * DEPTH) ** 0.25
LN_EPS = 1e-5
LOG2E = 1.4426950408889634

LANES = 128
SUBLANES = 8
MXU_DIM = 256
VMEM_LIMIT = 56 * 1024 * 1024

SEQ_TILE = 256
HG_CHUNK = 128
HG_SUB = 16
POOL_HALO = 16
ONES_ROWS = 2 * SUBLANES
DIGIT_BITS = 16
DIGIT_DTYPE = jnp.int16

_SEG_A = (0, Q_RANK + A_WIDTH)
_SEG_K = (_SEG_A[1], _SEG_A[1] + LANES)
_SEG_B = (_SEG_K[1], _SEG_K[1] + 4 * B_WIDTH)
IN_COLS_PADDED = _SEG_B[1]

INT_MIN = -2 ** 31
NEG_BIG = -1e30

bf16 = jnp.bfloat16
f32 = jnp.float32


def _dot(a, b):
    return jnp.dot(a, b, preferred_element_type=f32)


def _dot_nt(a, b):
    return lax.dot_general(a, b, (((1,), (1,)), ((), ())), preferred_element_type=f32)


def _dot_tn(a, b):
    return lax.dot_general(a, b, (((0,), (0,)), ((), ())), preferred_element_type=f32)


def _layer_norm(x, g, b):
    mu = jnp.mean(x, axis=-1, keepdims=True)
    xc = x - mu
    var = jnp.mean(xc * xc, axis=-1, keepdims=True)
    return xc * lax.rsqrt(var + LN_EPS) * g + b


def _const_spec(shape):
    nd = len(shape)
    return pl.BlockSpec(shape, lambda *_: (0,) * nd, pipeline_mode=pl.Buffered(1))


def _params(*sem):
    return pltpu.CompilerParams(dimension_semantics=sem, vmem_limit_bytes=VMEM_LIMIT)


def _inproj_kernel(h_ref, w_ref, wvt_ref, wwt_ref, cqg_ref, wuq_ref, wuqi_ref, kig_ref, kib_ref,
                   loglb_ref, log1mlb_ref,
                   qa_ref, ka_ref, vt_ref, qi_ref, klr_ref, wit_ref, qb_ref, logf_ref, lk_ref, ib_ref, gb_ref):
    tm = h_ref.shape[0]
    hb = h_ref[...].astype(bf16)

    pa = _dot(hb, w_ref[:, _SEG_A[0]:_SEG_A[1]])
    cq = pa[:, :Q_RANK]
    c = cq * lax.rsqrt(jnp.mean(cq * cq, axis=-1, keepdims=True) + LN_EPS) * cqg_ref[...]
    cb = c.astype(bf16)
    qa_ref[...] = (_dot(cb, wuq_ref[...]) * (A_HEAD_DIM ** -0.5)).astype(bf16)
    qi_ref[...] = _dot(cb, wuqi_ref[...]).astype(bf16)
    ka_ref[...] = pa[:, Q_RANK:].astype(bf16)
    vt_ref[...] = _dot_nt(wvt_ref[...], hb).astype(bf16)
    wit_ref[...] = _dot_nt(wwt_ref[...], hb) * (IDX_HEADS ** -0.5 * IDX_DIM ** -0.5)

    kd = _dot(hb, w_ref[:, _SEG_K[0]:_SEG_K[1]])
    left = lax.broadcasted_iota(jnp.int32, (1, LANES), 1) < IDX_DIM
    mu = jnp.sum(jnp.where(left, kd, 0.0), axis=-1, keepdims=True) * (1.0 / IDX_DIM)
    xc = kd - mu
    var = jnp.sum(jnp.where(left, xc * xc, 0.0), axis=-1, keepdims=True) * (1.0 / IDX_DIM)
    kn = xc * lax.rsqrt(var + LN_EPS) * kig_ref[...] + kib_ref[...]
    klr_ref[:tm, :] = jnp.where(left, kn, 0.0).astype(bf16)
    klr_ref[tm:, :] = jnp.where(left, 0.0, kn).astype(bf16)

    pb = _dot(hb, w_ref[:, _SEG_B[0]:_SEG_B[1]])
    z = pb[:, B_WIDTH:2 * B_WIDTH]
    e = jnp.exp(-jnp.abs(z))
    log_sig = jnp.minimum(z, 0.0) - jnp.log1p(e)
    a = loglb_ref[...]
    b = log1mlb_ref[...] + log_sig
    logf = jnp.maximum(a, b) + jnp.log1p(jnp.exp(-jnp.abs(a - b)))
    logk = log1mlb_ref[...] + (log_sig - z)
    for p in range(B_WIDTH // LANES):
        sl = slice(p * LANES, (p + 1) * LANES)
        qb_ref[p] = pb[:, sl]
        logf_ref[p] = logf[:, sl]
        lk_ref[p] = logk[:, sl]
        ib_ref[p] = pb[:, 2 * B_WIDTH + p * LANES:2 * B_WIDTH + (p + 1) * LANES]
        gb_ref[p] = pb[:, 3 * B_WIDTH + p * LANES:3 * B_WIDTH + (p + 1) * LANES]


def _inproj(h, w_cat, w_vt, w_wt, cq_gain, w_uq, w_uq_idx, kig2, kib2, loglb, log1mlb, *, tm):
    B, Lp, D = h.shape
    nt = Lp // tm
    row = lambda c: pl.BlockSpec((None, tm, c), lambda b, i: (b, i, 0))
    tile = lambda r, c: pl.BlockSpec((None, None, r, c), lambda b, i: (b, i, 0, 0))
    tok = lambda c, dt: (row(c), jax.ShapeDtypeStruct((B, Lp, c), dt))
    til = lambda r, c, dt: (tile(r, c), jax.ShapeDtypeStruct((B, nt, r, c), dt))
    npair = B_WIDTH // LANES
    pair = (pl.BlockSpec((None, npair, tm, LANES), lambda b, i: (b, 0, i, 0)),
            jax.ShapeDtypeStruct((B, npair, Lp, LANES), f32))
    outs = [tok(A_WIDTH, bf16), tok(A_WIDTH, bf16), til(A_WIDTH, tm, bf16), tok(IDX_HEADS * IDX_DIM, bf16),
            til(2 * tm, LANES, bf16), til(IDX_HEADS, tm, f32),
            pair, pair, pair, pair, pair]
    consts = [w_cat, w_vt, w_wt, cq_gain, w_uq, w_uq_idx, kig2, kib2, loglb, log1mlb]
    return pl.pallas_call(
        _inproj_kernel,
        grid=(B, nt),
        in_specs=[row(D)] + [_const_spec(c.shape) for c in consts],
        out_specs=[s for s, _ in outs],
        out_shape=[o for _, o in outs],
        compiler_params=_params("parallel", "parallel"),
        name="inproj",
    )(h, *consts)


def _tree(op, xs):
    xs = list(xs)
    while len(xs) > 1:
        xs = [op(xs[j], xs[j + 1]) for j in range(0, len(xs) - 1, 2)] + ([xs[-1]] if len(xs) % 2 else [])
    return xs[0]


def _f32_to_key(x):
    bits = pltpu.bitcast(x, jnp.int32)
    return bits ^ ((bits >> 31) & jnp.int32(0x7FFFFFFF))


TILE_UNROLLS = (4, 2, 1)


def _key_to_f32(key):
    return pltpu.bitcast(key ^ ((key >> 31) & jnp.int32(0x7FFFFFFF)), f32)


def _pair_loop(n, body, carry):
    start = 0
    for unroll in TILE_UNROLLS:
        def trip(j, c, start=start, unroll=unroll):
            for u in range(unroll):
                c = body(start + unroll * j + u, c)
            return c

        trips = (n - start) // unroll
        carry = lax.fori_loop(0, trips, trip, carry)
        start = start + unroll * trips
    return carry


def _dsa_kernel(qa_ref, qi_ref, wit_ref, klr_ref, ka_ref, vt_ref, pos_ref, o_ref,
                keys_scr, sc_scr, digit_scr, thr_scr, exc_scr, bias_scr, sacc_scr, qaug_scr, pm_scr, acc_scr, out_scr,
                *, topk, tq, seq_len):
    i = pl.program_id(1)
    nkt = i + 1
    lane = lax.broadcasted_iota(jnp.int32, (1, LANES), 1)
    left = lane < A_HEAD_DIM
    q_pos = i * tq + lax.broadcasted_iota(jnp.int32, (1, tq), 1)
    k_iota = lax.broadcasted_iota(jnp.int32, (tq, 1), 0)

    def alibi_coef(h):
        slope = 2.0 ** -(h + 1)
        return jnp.where(lane == 0, slope * tq, jnp.where(lane == 1, slope, 0.0))

    keep = (jnp.where(left, 1.0, 0.0).astype(bf16), jnp.where(left, 0.0, 1.0).astype(bf16))
    for h in range(A_HEADS):
        qaug_scr[h, :, :LANES] = qa_ref[:, (h // 2) * LANES:(h // 2 + 1) * LANES] * keep[h % 2]
        qaug_scr[h, :, LANES:] = jnp.broadcast_to(alibi_coef(h).astype(bf16), (tq, LANES))

    def score_body(kt, carry):
        klr = klr_ref[kt]
        for p in range(IDX_HEADS // 2):
            s = _dot_nt(klr, qi_ref[:, p * LANES:(p + 1) * LANES])
            t = (jnp.maximum(s[:tq], 0.0) * wit_ref[2 * p:2 * p + 1, :]
                 + jnp.maximum(s[tq:], 0.0) * wit_ref[2 * p + 1:2 * p + 2, :])
            if p == 0:
                sacc_scr[...] = t
            else:
                sacc_scr[...] += t
        causal = (kt * tq + k_iota) <= q_pos
        sc_scr[kt] = jnp.where(causal, sacc_scr[...], -jnp.inf)
        keys_scr[kt] = jnp.where(causal, _f32_to_key(sacc_scr[...]), jnp.int32(INT_MIN))
        return carry

    _pair_loop(nkt, score_body, 0)

    w = DIGIT_BITS
    pack = SUBLANES * (32 // w)
    half = 2 ** (w - 1)

    def kth_largest_digit(n_floor):
        def bit_body(it, carry):
            c, n_c = carry
            trial = c + jnp.left_shift(jnp.int32(1), w - 1 - it)
            trial_b = jnp.broadcast_to(trial, (pack, tq)).astype(DIGIT_DTYPE)

            def cnt_body(kt, a):
                return a + _tree(jnp.add, ((digit_scr[kt, r * pack:(r + 1) * pack, :] >= trial_b).astype(DIGIT_DTYPE)
                                           for r in range(tq // pack)))

            a = _pair_loop(nkt, cnt_body, jnp.zeros((pack, tq), DIGIT_DTYPE))
            n = jnp.sum(a.astype(jnp.int32), axis=0, keepdims=True)
            return jnp.where(n >= topk, trial, c), jnp.where(n >= topk, n, n_c)

        return lax.fori_loop(0, w, bit_body, (jnp.full((1, tq), -half, jnp.int32), n_floor))

    prefix = None
    n_ge = jnp.full((1, tq), 1, jnp.int32) * (nkt * tq)
    for stage in range(32 // w):
        shift = 32 - (stage + 1) * w

        def digit_body(kt, carry):
            if prefix is None:
                val = keys_scr[kt] >> shift
            else:
                lo = prefix << (shift + w)
                hi = lo + jnp.int32(2 ** (shift + w) - 1)
                val = ((jnp.clip(keys_scr[kt], lo, hi) - lo) >> shift) - half
            digit_scr[kt] = val.astype(DIGIT_DTYPE)
            return carry

        lax.fori_loop(0, nkt, digit_body, 0)
        digit, n_ge = kth_largest_digit(n_ge)
        prefix = digit if prefix is None else (prefix << w) + (digit + half)
    def count_scores_ge(trial):
        trial_b = jnp.broadcast_to(trial, (SUBLANES, tq))

        def cnt_body(kt, a):
            return a + _tree(jnp.add, ((sc_scr[kt, r * SUBLANES:(r + 1) * SUBLANES, :] >= trial_b).astype(jnp.int32)
                                       for r in range(tq // SUBLANES)))

        return jnp.sum(_pair_loop(nkt, cnt_body, jnp.zeros((SUBLANES, tq), jnp.int32)), axis=0, keepdims=True)

    def set_threshold(key, n_at_key):
        found = key > jnp.int32(INT_MIN)
        thr_scr[...] = jnp.where(found, _key_to_f32(key), jnp.finfo(f32).min)
        exc_scr[...] = jnp.where(jnp.logical_and(found, q_pos < seq_len), n_at_key - topk, 0)
        return found

    found = set_threshold(prefix, n_ge)

    def write_bias(thr):
        def bias_body(kt, a):
            ge = sc_scr[kt] >= thr
            bias_scr[kt] = jnp.where(ge, 0.0, NEG_BIG)
            return a + _tree(jnp.add, (ge.astype(jnp.int32)[r * SUBLANES:(r + 1) * SUBLANES]
                                       for r in range(tq // SUBLANES)))

        return jnp.sum(lax.fori_loop(0, nkt, bias_body, jnp.zeros((SUBLANES, tq), jnp.int32)), axis=0, keepdims=True)

    mismatch = jnp.logical_and(found, write_bias(thr_scr[...]) != n_ge)

    @pl.when(jnp.max(mismatch.astype(jnp.int32)) > 0)
    def _():
        def bit_body(it, carry):
            c, n_c = carry
            trial = c + jnp.left_shift(jnp.int32(1), 31 - it)
            n = count_scores_ge(_key_to_f32(trial))
            return jnp.where(n >= topk, trial, c), jnp.where(n >= topk, n, n_c)

        c, n_c = lax.fori_loop(0, 32, bit_body, (jnp.full((1, tq), INT_MIN, jnp.int32),
                                                 jnp.full((1, tq), 1, jnp.int32) * (nkt * tq)))
        set_threshold(c, n_c)
        write_bias(thr_scr[...])

    thr = thr_scr[...]
    excess = exc_scr[...]
    has_ties = jnp.max(excess) > 0
    pos_bits = (keys_scr.shape[0] * tq - 1).bit_length()

    @pl.when(has_ties)
    def _():
        def tied_above(cut):
            def cnt_body(kt, a):
                tied = jnp.logical_and(sc_scr[kt] == thr, (kt * tq + k_iota) > cut)
                return a + _tree(jnp.add, (x for x in (tied.astype(jnp.int32)[r * SUBLANES:(r + 1) * SUBLANES]
                                                       for r in range(tq // SUBLANES))))

            a = lax.fori_loop(0, nkt, cnt_body, jnp.zeros((SUBLANES, tq), jnp.int32))
            return jnp.sum(a, axis=0, keepdims=True)

        def cut_body(it, cut):
            trial = cut | jnp.left_shift(jnp.int32(1), pos_bits - 1 - it)
            return jnp.where(tied_above(trial) >= excess, trial, cut)

        cut = lax.fori_loop(0, pos_bits, cut_body, jnp.zeros((1, tq), jnp.int32))

        def bias_body(kt, carry):
            key = sc_scr[kt]
            keep_tied = jnp.where((kt * tq + k_iota) <= cut, 0.0, NEG_BIG)
            bias_scr[kt] = jnp.where(key > thr, 0.0, jnp.where(key == thr, keep_tied, NEG_BIG))
            return carry

        lax.fori_loop(0, nkt, bias_body, 0)


    groups = tq // SUBLANES
    row_groups = lambda x: (x[r * SUBLANES:(r + 1) * SUBLANES] for r in range(groups))

    def logits(kt, h):
        k0 = pl.multiple_of(kt * tq, tq)
        kaug = jnp.concatenate([ka_ref[pl.ds(k0, tq), (h // 2) * LANES:(h // 2 + 1) * LANES],
                                pos_ref[pl.ds(k0, tq), :]], axis=1)
        return _dot_nt(kaug, qaug_scr[h]) + bias_scr[kt]

    def max_body(kt, m8):
        return tuple(jnp.maximum(m8[h], _tree(jnp.maximum, row_groups(logits(kt, h)))) for h in range(A_HEADS))

    m8 = _pair_loop(nkt, max_body, (jnp.full((SUBLANES, tq), NEG_BIG, f32),) * A_HEADS)

    m_cols = jnp.concatenate([jnp.max(m8[h], axis=0, keepdims=True) for h in range(A_HEADS)], axis=0).T
    for h in range(A_HEADS):
        m_col = m_cols[:, h:h + 1]
        m_hi = m_col.astype(bf16).astype(f32)
        coef = jnp.where(lane == 2, -m_hi, jnp.where(lane == 3, m_hi - m_col, alibi_coef(h)))
        qaug_scr[h, :, LANES:] = coef.astype(bf16)

    acc_scr[...] = jnp.zeros(acc_scr.shape, f32)
    ones = jnp.ones((ONES_ROWS, tq), bf16)

    def acc_body(kt, carry):
        for h in range(A_HEADS):
            pm_scr[h] = jnp.exp(logits(kt, h)).astype(bf16)
        for h in range(A_HEADS):
            rows = slice(h * (A_HEAD_DIM + ONES_ROWS), (h + 1) * (A_HEAD_DIM + ONES_ROWS))
            v_ones = jnp.concatenate([vt_ref[kt, h * A_HEAD_DIM:(h + 1) * A_HEAD_DIM, :], ones], axis=0)
            acc_scr[rows, :] += _dot(v_ones, pm_scr[h])
        return carry

    _pair_loop(nkt, acc_body, 0)
    for h in range(A_HEADS):
        r0 = h * (A_HEAD_DIM + ONES_ROWS)
        out_scr[h * A_HEAD_DIM:(h + 1) * A_HEAD_DIM, :] = (acc_scr[r0:r0 + A_HEAD_DIM, :]
                                                           / acc_scr[r0 + A_HEAD_DIM:r0 + A_HEAD_DIM + 1, :])
    o_ref[...] = out_scr[...].T.astype(o_ref.dtype)


def _dsa(qa, qi, wit, klr, ka, vt, *, topk, tq, seq_len):
    B, Lp, _ = qa.shape
    nq = Lp // tq
    assert nq * (tq // (SUBLANES * (32 // DIGIT_BITS))) < 2 ** (DIGIT_BITS - 1)
    kpos = np.arange(Lp)
    posfeat = np.zeros((Lp, LANES), np.float32)
    posfeat[:, 0] = kpos // tq
    posfeat[:, 1] = kpos % tq
    posfeat[:, 2:4] = 1.0
    posfeat = jnp.asarray(posfeat, bf16)
    qrow = lambda c: pl.BlockSpec((None, tq, c), lambda b, i: (b, i, 0))
    seq = lambda *s: pl.BlockSpec((None,) + s, lambda b, i: (b,) + (0,) * len(s), pipeline_mode=pl.Buffered(1))
    return pl.pallas_call(
        functools.partial(_dsa_kernel, topk=topk, tq=tq, seq_len=seq_len),
        grid=(B, nq),
        in_specs=[qrow(A_WIDTH), qrow(IDX_HEADS * IDX_DIM),
                  pl.BlockSpec((None, None, IDX_HEADS, tq), lambda b, i: (b, i, 0, 0)),
                  seq(nq, 2 * tq, LANES), seq(Lp, A_WIDTH), seq(nq, A_WIDTH, tq), _const_spec(posfeat.shape)],
        out_specs=qrow(A_WIDTH),
        out_shape=jax.ShapeDtypeStruct((B, Lp, A_WIDTH), bf16),
        scratch_shapes=[
            pltpu.VMEM((nq, tq, tq), jnp.int32),
            pltpu.VMEM((nq, tq, tq), f32),
            pltpu.VMEM((nq, tq, tq), DIGIT_DTYPE),
            pltpu.VMEM((1, tq), f32),
            pltpu.VMEM((1, tq), jnp.int32),
            pltpu.VMEM((nq, tq, tq), f32),
            pltpu.VMEM((tq, tq), f32),
            pltpu.VMEM((A_HEADS, tq, 2 * LANES), bf16),
            pltpu.VMEM((A_HEADS, tq, tq), bf16),
            pltpu.VMEM((A_HEADS * (A_HEAD_DIM + ONES_ROWS), tq), f32),
            pltpu.VMEM((A_WIDTH, tq), f32),
        ],
        compiler_params=_params("parallel", "arbitrary"),
        name="dsa",
    )(qa, qi, wit, klr, ka, vt, posfeat)


def _hgrn2_offsets(C):
    base = [HG_SUB * (t // HG_SUB) for t in range(C)]
    ns = [SUBLANES * ((t - base[t]) // SUBLANES + 1) for t in range(C)]
    return base, ns, np.concatenate([[0], np.cumsum(ns)]).tolist()


def _hgrn2_kernel(qb_ref, logf_ref, lk_ref, ib_ref, gb_ref, gain_ref, tril_ref, bones_ref, o_ref,
                  st_scr, a_scr, w_scr, intra_scr):
    C = HG_CHUNK
    P = qb_ref.shape[0]
    pairs = range(P)
    base, ns, off = _hgrn2_offsets(C)

    @pl.when(pl.program_id(1) == 0)
    def _():
        st_scr[...] = jnp.zeros(st_scr.shape, f32)

    tril = tril_ref[...]
    bones = bones_ref[...]

    def cumsum(lf):
        hi = lf.astype(bf16)
        r1 = lf - hi.astype(f32)
        mid = r1.astype(bf16)
        lo = (r1 - mid.astype(f32)).astype(bf16)
        return _dot(tril, hi) + _dot(tril, mid) + _dot(tril, lo)

    b2 = [cumsum(logf_ref[p]) * LOG2E for p in pairs]
    c2 = [b2[p] - lk_ref[p] * LOG2E for p in pairs]
    q = [qb_ref[p] for p in pairs]
    v = [ib_ref[p] for p in pairs]

    inter = [_dot_nt((q[p] * jnp.exp2(b2[p])).astype(bf16), st_scr[p].astype(bf16)) for p in pairs]
    for p in pairs:
        b_last = b2[p][C - 1:C, :]
        k_dec = jnp.exp2(b_last - c2[p]).astype(bf16)
        upd = _dot_tn(v[p].astype(bf16), k_dec)
        st_scr[p] = st_scr[p] * jnp.exp2(b_last) + upd * bones.astype(f32)

    left = lax.broadcasted_iota(jnp.int32, (1, LANES), 1) < B_KEY_DIM
    srow = lax.broadcasted_iota(jnp.int32, (C, 1), 0)
    blocks = range(1, C // HG_SUB)
    scores = {}
    for p in pairs:
        for blk in blocks:
            t0 = blk * HG_SUB
            b_r = b2[p][t0 - 1:t0, :]
            q_dec = q[p][t0:t0 + HG_SUB] * jnp.exp2(b2[p][t0:t0 + HG_SUB] - b_r)
            k_dec = jnp.exp2(jnp.where(srow < t0, b_r - c2[p], -jnp.inf)).astype(bf16)
            lhs = jnp.concatenate([jnp.where(left, q_dec, 0.0), jnp.where(left, 0.0, q_dec)], axis=0).astype(bf16)
            scores[p, blk] = _dot_nt(lhs, k_dec).astype(bf16)
    early = []
    for p in pairs:
        vb = v[p].astype(bf16)
        parts = [jnp.zeros((HG_SUB, LANES), f32)]
        for blk in blocks:
            res = _dot(scores[p, blk], vb)
            parts.append(jnp.where(left, res[:HG_SUB], res[HG_SUB:]))
        early.append(jnp.concatenate(parts, axis=0))

    sidx = lax.broadcasted_iota(jnp.int32, (SUBLANES, 1), 0)
    for p in pairs:
        for t in range(C):
            s0, n = base[t], ns[t]
            diff = b2[p][t:t + 1, :] - c2[p][s0:s0 + n]
            last = jnp.where(sidx <= t % SUBLANES, diff[n - SUBLANES:], -jnp.inf)
            diff = last if n == SUBLANES else jnp.concatenate([diff[:n - SUBLANES], last], axis=0)
            a_scr[p, off[t]:off[t] + n, :] = q[p][t:t + 1, :] * jnp.exp2(diff)
    for p in pairs:
        w_scr[p] = _dot(a_scr[p].astype(bf16), bones)
    for p in pairs:
        for t in range(C):
            s0, n = base[t], ns[t]
            intra_scr[p, t:t + 1, :] = jnp.sum(w_scr[p, off[t]:off[t] + n, :] * v[p][s0:s0 + n], axis=0,
                                               keepdims=True)

    o = [inter[p] + early[p] + intra_scr[p] for p in pairs]
    ss = []
    for p in pairs:
        o2 = o[p] * o[p]
        o2_hi = o2.astype(bf16)
        o2_lo = (o2 - o2_hi.astype(f32)).astype(bf16)
        ss.append(_dot(o2_hi, bones) + _dot(o2_lo, bones))
    for p in pairs:
        on = o[p] * lax.rsqrt(ss[p] * (1.0 / B_KEY_DIM) + LN_EPS) * gain_ref[p]
        g = gb_ref[p]
        o_ref[p] = (on * (g / (1.0 + jnp.exp(-g)))).astype(o_ref.dtype)


def _hgrn2(qb, logf, logk, ib, gb, gain):
    B, P, Lp, W = qb.shape
    C = HG_CHUNK
    head = np.arange(W) // B_KEY_DIM
    tril = jnp.asarray(np.tril(np.ones((C, C), np.float32)), bf16)
    bones = jnp.asarray((head[:, None] == head[None, :]).astype(np.float32), bf16)
    rows = _hgrn2_offsets(C)[2][-1]
    row = pl.BlockSpec((None, P, C, W), lambda b, c: (b, 0, c, 0))
    return pl.pallas_call(
        _hgrn2_kernel,
        grid=(B, Lp // C),
        in_specs=[row] * 5 + [_const_spec(gain.shape), _const_spec(tril.shape), _const_spec(bones.shape)],
        out_specs=row,
        out_shape=jax.ShapeDtypeStruct((B, P, Lp, W), bf16),
        scratch_shapes=[pltpu.VMEM((P, W, W), f32), pltpu.VMEM((P, rows, W), f32), pltpu.VMEM((P, rows, W), f32),
                        pltpu.VMEM((P, C, W), f32)],
        compiler_params=_params("parallel", "arbitrary"),
        name="hgrn2",
    )(qb, logf, logk, ib, gb, gain, tril, bones)


def _ffn_tail(h, mix, g1_ref, b1_ref, wg_ref, wu_ref, wd_ref, g2_ref, b2_ref, out_ref):
    h1 = _layer_norm(ALPHA * h + mix, g1_ref[...], b1_ref[...])
    hb = h1.astype(bf16)
    g = _dot(hb, wg_ref[...])
    u = _dot(hb, wu_ref[...])
    act = (g / (1.0 + jnp.exp(-g))) * u
    f = _dot(act.astype(bf16), wd_ref[...])
    out_ref[...] = _layer_norm(ALPHA * h1 + f, g2_ref[...], b2_ref[...])


def _post_even_kernel(h_ref, oa_ref, ob_ref, wout_ref, *rest):
    ob = jnp.concatenate([ob_ref[p] for p in range(B_WIDTH // LANES)], axis=1)
    mix = _dot(oa_ref[...], wout_ref[:A_WIDTH, :]) + _dot(ob, wout_ref[A_WIDTH:, :])
    _ffn_tail(h_ref[...], mix, *rest)


def _post_odd_kernel(h_ref, prev_ref, wpool_ref, pscale_ref, *rest, tm, skip):
    row0 = skip + pl.program_id(1) * tm
    h = h_ref[...].reshape(h_ref.shape[-2:])
    prev = jnp.where(row0 == 0, 0.0, prev_ref[...])
    x = jnp.concatenate([prev, h], axis=0)
    pos = row0 + lax.broadcasted_iota(jnp.int32, (tm, 1), 0)
    parts = []
    for g, win in enumerate(POOL_WINDOWS):
        sl = slice(g * POOL_GROUP_DIM, (g + 1) * POOL_GROUP_DIM)
        s = x[:, sl]
        span = 1
        while span < win:
            s = s + pltpu.roll(s, span, axis=0)
            span *= 2
        cnt = jnp.minimum(pos + 1, win).astype(f32)
        d = s[POOL_HALO:, :] / cnt - h[:, sl]
        parts.append(_dot(d.astype(bf16), wpool_ref[g]))
    mix = jnp.concatenate(parts, axis=-1) * pscale_ref[...]
    _ffn_tail(h, mix, *rest)


POST_TILE_MAX = 576


def _post_tile(Lp):
    return max(t for t in range(POOL_HALO, POST_TILE_MAX + 1, POOL_HALO) if Lp % t == 0)


def _post(h, mixer_inputs, mixer_specs, kernel, tail_consts, *, tm, skip=0, out_len=None):
    B, Lp, D = h.shape
    out_len = Lp if out_len is None else out_len
    row = pl.BlockSpec((None, tm, D), lambda b, i: (b, i, 0))
    h_rows = row if skip == 0 else pl.BlockSpec((pl.Element(1), pl.Element(tm), pl.Element(D)),
                                                lambda b, i: (b, pl.multiple_of(skip + i * tm, POOL_HALO), 0))
    return pl.pallas_call(
        kernel,
        grid=(B, out_len // tm),
        in_specs=[h_rows] + mixer_specs + [_const_spec(c.shape) for c in tail_consts],
        out_specs=row,
        out_shape=jax.ShapeDtypeStruct((B, out_len, D), f32),
        compiler_params=_params("parallel", "parallel"),
        name=getattr(kernel, "__name__", "post_odd"),
    )(h, *mixer_inputs, *tail_consts)


def _post_even(h, oa, ob, w_out, tail_consts, *, tm):
    row = lambda c: pl.BlockSpec((None, tm, c), lambda b, i: (b, i, 0))
    pairs = pl.BlockSpec((None, B_WIDTH // LANES, tm, LANES), lambda b, i: (b, 0, i, 0))
    return _post(h, [oa, ob, w_out], [row(A_WIDTH), pairs, _const_spec(w_out.shape)],
                 _post_even_kernel, tail_consts, tm=tm)


def _post_odd(h, w_pool, pool_scale, tail_consts, *, tm, skip=0, out_len=None):
    D = h.shape[-1]
    assert tm % POOL_HALO == 0 and skip % POOL_HALO == 0
    prev = pl.BlockSpec((None, POOL_HALO, D),
                        lambda b, i: (b, jnp.maximum((skip + i * tm) // POOL_HALO - 1, 0), 0))
    kern = functools.partial(_post_odd_kernel, tm=tm, skip=skip)
    kern.__name__ = "post_odd"
    return _post(h, [h, w_pool, pool_scale], [prev, _const_spec(w_pool.shape), _const_spec(pool_scale.shape)],
                 kern, tail_consts, tm=tm, skip=skip, out_len=out_len)


_W_IN_SPLITS = np.cumsum([0, Q_RANK, A_WIDTH, A_WIDTH, IDX_DIM, IDX_HEADS, B_WIDTH, B_WIDTH, B_WIDTH, B_WIDTH])


def _w_in_seg(w, j):
    return w[:, _W_IN_SPLITS[j]:_W_IN_SPLITS[j + 1]]


def _pack_w_in(w):
    segs = [_w_in_seg(w, j) for j in (0, 1, 3, 3, 5, 6, 7, 8)]
    return jnp.concatenate(segs, axis=1).astype(bf16)


def kernel(x, meta_tokens, w_in, cq_gain, w_uq, w_uq_idx, kidx_gain, kidx_bias, lb_raw, onorm_gain, w_out,
           w_pool, pool_scale, ln_mix_g, ln_mix_b, w_gate, w_up, w_down, ln_ffn_g, ln_ffn_b):
    B, S, D = x.shape
    L = N_META + S
    Lp = -(-L // SEQ_TILE) * SEQ_TILE
    topk = min(TOPK_MAX, S // 4)
    tm = SEQ_TILE

    meta = jnp.broadcast_to(meta_tokens.astype(x.dtype)[None], (B, N_META, D))
    h = jnp.concatenate([meta, x, jnp.zeros((B, Lp - L, D), x.dtype)], axis=1)

    lower = jnp.cumsum(jax.nn.softmax(lb_raw.astype(f32), axis=0), axis=0)
    lower = lower - lower[:1]

    row = lambda v: v.reshape(1, -1).astype(f32)
    for layer in range(DEPTH):
        j = layer // 2
        tail = [row(ln_mix_g[layer]), row(ln_mix_b[layer]), w_gate[layer].astype(bf16), w_up[layer].astype(bf16),
                w_down[layer].astype(bf16), row(ln_ffn_g[layer]), row(ln_ffn_b[layer])]
        if layer % 2 == 0:
            lb = lower[j]
            qa, ka, vt, qi, klr, wit, qb, logf, logk, ib, gb = _inproj(
                h, _pack_w_in(w_in[j]), _w_in_seg(w_in[j], 2).T.astype(bf16), _w_in_seg(w_in[j], 4).T.astype(bf16),
                row(cq_gain[j]), w_uq[j].astype(bf16), w_uq_idx[j].astype(bf16),
                row(jnp.tile(kidx_gain[j], 2)), row(jnp.tile(kidx_bias[j], 2)),
                row(jnp.log(lb)), row(jnp.log1p(-lb)), tm=tm)
            oa = _dsa(qa, qi, wit, klr, ka, vt, topk=topk, tq=SEQ_TILE, seq_len=L)
            ob = _hgrn2(qb, logf, logk, ib, gb, onorm_gain[j].astype(f32).reshape(-1, 1, LANES))
            h = _post_even(h, oa, ob, w_out[j].astype(bf16), tail, tm=_post_tile(Lp))
        elif layer < DEPTH - 1:
            h = _post_odd(h, w_pool[j].astype(bf16), row(pool_scale[j]), tail, tm=_post_tile(Lp))
        else:
            h = _post_odd(h, w_pool[j].astype(bf16), row(pool_scale[j]), tail, tm=_post_tile(S), skip=N_META, out_len=S)
    return h
```

```python
import functools

import jax
import jax.numpy as jnp
import numpy as np
from jax import lax
from jax.experimental import pallas as pl
from jax.experimental.pallas import tpu as pltpu

D_MODEL = 1024
DEPTH = 4
N_META = 16
A_HEADS = 8
A_HEAD_DIM = 64
A_WIDTH = A_HEADS * A_HEAD_DIM
Q_RANK = 256
IDX_HEADS = 16
IDX_DIM = 64
TOPK_MAX = 256
B_HEADS = 8
B_KEY_DIM = 64
B_WIDTH = B_HEADS * B_KEY_DIM
POOL_WINDOWS = (2, 4, 8, 16)
POOL_GROUP_DIM = D_MODEL // len(POOL_WINDOWS)
D_FF = -(-8 * D_MODEL // (3 * 256)) * 256
ALPHA = (2 * DEPTH) ** 0.25
LN_EPS = 1e-5
LOG2E = 1.4426950408889634

LANES = 128
SUBLANES = 8
MXU_DIM = 256
VMEM_LIMIT = 56 * 1024 * 1024

SEQ_TILE = 256
HG_CHUNK = 128
HG_SUB = 16
POOL_HALO = 16
ONES_ROWS = 2 * SUBLANES
DIGIT_BITS = 16
DIGIT_DTYPE = jnp.int16

_SEG_A = (0, Q_RANK + A_WIDTH)
_SEG_K = (_SEG_A[1], _SEG_A[1] + LANES)
_SEG_B = (_SEG_K[1], _SEG_K[1] + 4 * B_WIDTH)
IN_COLS_PADDED = _SEG_B[1]

INT_MIN = -2 ** 31
NEG_BIG = -1e30

bf16 = jnp.bfloat16
f32 = jnp.float32


def _dot(a, b):
    return jnp.dot(a, b, preferred_element_type=f32)


def _dot_nt(a, b):
    return lax.dot_general(a, b, (((1,), (1,)), ((), ())), preferred_element_type=f32)


def _dot_tn(a, b):
    return lax.dot_general(a, b, (((0,), (0,)), ((), ())), preferred_element_type=f32)


def _layer_norm(x, g, b):
    mu = jnp.mean(x, axis=-1, keepdims=True)
    xc = x - mu
    var = jnp.mean(xc * xc, axis=-1, keepdims=True)
    return xc * lax.rsqrt(var + LN_EPS) * g + b


def _const_spec(shape):
    nd = len(shape)
    return pl.BlockSpec(shape, lambda *_: (0,) * nd, pipeline_mode=pl.Buffered(1))


def _params(*sem):
    return pltpu.CompilerParams(dimension_semantics=sem, vmem_limit_bytes=VMEM_LIMIT)


def _inproj_kernel(h_ref, w_ref, wvt_ref, wwt_ref, cqg_ref, wuq_ref, wuqi_ref, kig_ref, kib_ref,
                   loglb_ref, log1mlb_ref,
                   qa_ref, ka_ref, vt_ref, qi_ref, klr_ref, wit_ref, qb_ref, logf_ref, lk_ref, ib_ref, gb_ref):
    tm = h_ref.shape[0]
    hb = h_ref[...].astype(bf16)

    pa = _dot(hb, w_ref[:, _SEG_A[0]:_SEG_A[1]])
    cq = pa[:, :Q_RANK]
    c = cq * lax.rsqrt(jnp.mean(cq * cq, axis=-1, keepdims=True) + LN_EPS) * cqg_ref[...]
    cb = c.astype(bf16)
    qa_ref[...] = (_dot(cb, wuq_ref[...]) * (A_HEAD_DIM ** -0.5)).astype(bf16)
    qi_ref[...] = _dot(cb, wuqi_ref[...]).astype(bf16)
    ka_ref[...] = pa[:, Q_RANK:].astype(bf16)
    vt_ref[...] = _dot_nt(wvt_ref[...], hb).astype(bf16)
    wit_ref[...] = _dot_nt(wwt_ref[...], hb) * (IDX_HEADS ** -0.5 * IDX_DIM ** -0.5)

    kd = _dot(hb, w_ref[:, _SEG_K[0]:_SEG_K[1]])
    left = lax.broadcasted_iota(jnp.int32, (1, LANES), 1) < IDX_DIM
    mu = jnp.sum(jnp.where(left, kd, 0.0), axis=-1, keepdims=True) * (1.0 / IDX_DIM)
    xc = kd - mu
    var = jnp.sum(jnp.where(left, xc * xc, 0.0), axis=-1, keepdims=True) * (1.0 / IDX_DIM)
    kn = xc * lax.rsqrt(var + LN_EPS) * kig_ref[...] + kib_ref[...]
    klr_ref[:tm, :] = jnp.where(left, kn, 0.0).astype(bf16)
    klr_ref[tm:, :] = jnp.where(left, 0.0, kn).astype(bf16)

    pb = _dot(hb, w_ref[:, _SEG_B[0]:_SEG_B[1]])
    z = pb[:, B_WIDTH:2 * B_WIDTH]
    e = jnp.exp(-jnp.abs(z))
    log_sig = jnp.minimum(z, 0.0) - jnp.log1p(e)
    a = loglb_ref[...]
    b = log1mlb_ref[...] + log_sig
    logf = jnp.maximum(a, b) + jnp.log1p(jnp.exp(-jnp.abs(a - b)))
    logk = log1mlb_ref[...] + (log_sig - z)
    for p in range(B_WIDTH // LANES):
        sl = slice(p * LANES, (p + 1) * LANES)
        qb_ref[p] = pb[:, sl]
        logf_ref[p] = logf[:, sl]
        lk_ref[p] = logk[:, sl]
        ib_ref[p] = pb[:, 2 * B_WIDTH + p * LANES:2 * B_WIDTH + (p + 1) * LANES]
        gb_ref[p] = pb[:, 3 * B_WIDTH + p * LANES:3 * B_WIDTH + (p + 1) * LANES]


def _inproj(h, w_cat, w_vt, w_wt, cq_gain, w_uq, w_uq_idx, kig2, kib2, loglb, log1mlb, *, tm):
    B, Lp, D = h.shape
    nt = Lp // tm
    row = lambda c: pl.BlockSpec((None, tm, c), lambda b, i: (b, i, 0))
    tile = lambda r, c: pl.BlockSpec((None, None, r, c), lambda b, i: (b, i, 0, 0))
    tok = lambda c, dt: (row(c), jax.ShapeDtypeStruct((B, Lp, c), dt))
    til = lambda r, c, dt: (tile(r, c), jax.ShapeDtypeStruct((B, nt, r, c), dt))
    npair = B_WIDTH // LANES
    pair = (pl.BlockSpec((None, npair, tm, LANES), lambda b, i: (b, 0, i, 0)),
            jax.ShapeDtypeStruct((B, npair, Lp, LANES), f32))
    outs = [tok(A_WIDTH, bf16), tok(A_WIDTH, bf16), til(A_WIDTH, tm, bf16), tok(IDX_HEADS * IDX_DIM, bf16),
            til(2 * tm, LANES, bf16), til(IDX_HEADS, tm, f32),
            pair, pair, pair, pair, pair]
    consts = [w_cat, w_vt, w_wt, cq_gain, w_uq, w_uq_idx, kig2, kib2, loglb, log1mlb]
    return pl.pallas_call(
        _inproj_kernel,
        grid=(B, nt),
        in_specs=[row(D)] + [_const_spec(c.shape) for c in consts],
        out_specs=[s for s, _ in outs],
        out_shape=[o for _, o in outs],
        compiler_params=_params("parallel", "parallel"),
        name="inproj",
    )(h, *consts)


def _tree(op, xs):
    xs = list(xs)
    while len(xs) > 1:
        xs = [op(xs[j], xs[j + 1]) for j in range(0, len(xs) - 1, 2)] + ([xs[-1]] if len(xs) % 2 else [])
    return xs[0]


def _f32_to_key(x):
    bits = pltpu.bitcast(x, jnp.int32)
    return bits ^ ((bits >> 31) & jnp.int32(0x7FFFFFFF))


TILE_UNROLLS = (8, 4, 2, 1)


def _key_to_f32(key):
    return pltpu.bitcast(key ^ ((key >> 31) & jnp.int32(0x7FFFFFFF)), f32)


def _pair_loop(n, body, carry):
    start = 0
    for unroll in TILE_UNROLLS:
        def trip(j, c, start=start, unroll=unroll):
            for u in range(unroll):
                c = body(start + unroll * j + u, c)
            return c

        trips = (n - start) // unroll
        carry = lax.fori_loop(0, trips, trip, carry)
        start = start + unroll * trips
    return carry


def _dsa_kernel(qa_ref, qi_ref, wit_ref, klr_ref, ka_ref, vt_ref, pos_ref, o_ref,
                keys_scr, sc_scr, digit_scr, thr_scr, exc_scr, bias_scr, sacc_scr, qaug_scr, pm_scr, acc_scr, out_scr,
                *, topk, tq, seq_len):
    i = pl.program_id(1)
    nkt = i + 1
    lane = lax.broadcasted_iota(jnp.int32, (1, LANES), 1)
    left = lane < A_HEAD_DIM
    q_pos = i * tq + lax.broadcasted_iota(jnp.int32, (1, tq), 1)
    k_iota = lax.broadcasted_iota(jnp.int32, (tq, 1), 0)

    def alibi_coef(h):
        slope = 2.0 ** -(h + 1)
        return jnp.where(lane == 0, slope * tq, jnp.where(lane == 1, slope, 0.0))

    keep = (jnp.where(left, 1.0, 0.0).astype(bf16), jnp.where(left, 0.0, 1.0).astype(bf16))
    for h in range(A_HEADS):
        qaug_scr[h, :, :LANES] = qa_ref[:, (h // 2) * LANES:(h // 2 + 1) * LANES] * keep[h % 2]
        qaug_scr[h, :, LANES:] = jnp.broadcast_to(alibi_coef(h).astype(bf16), (tq, LANES))

    def score_body(kt, carry):
        klr = klr_ref[kt]
        for p in range(IDX_HEADS // 2):
            s = _dot_nt(klr, qi_ref[:, p * LANES:(p + 1) * LANES])
            t = (jnp.maximum(s[:tq], 0.0) * wit_ref[2 * p:2 * p + 1, :]
                 + jnp.maximum(s[tq:], 0.0) * wit_ref[2 * p + 1:2 * p + 2, :])
            if p == 0:
                sacc_scr[...] = t
            else:
                sacc_scr[...] += t
        causal = (kt * tq + k_iota) <= q_pos
        sc_scr[kt] = jnp.where(causal, sacc_scr[...], -jnp.inf)
        keys_scr[kt] = jnp.where(causal, _f32_to_key(sacc_scr[...]), jnp.int32(INT_MIN))
        return carry

    _pair_loop(nkt, score_body, 0)

    w = DIGIT_BITS
    pack = SUBLANES * (32 // w)
    half = 2 ** (w - 1)

    def kth_largest_digit(n_floor):
        def bit_body(it, carry):
            c, n_c = carry
            trial = c + jnp.left_shift(jnp.int32(1), w - 1 - it)
            trial_b = jnp.broadcast_to(trial, (pack, tq)).astype(DIGIT_DTYPE)

            def cnt_body(kt, a):
                return a + _tree(jnp.add, ((digit_scr[kt, r * pack:(r + 1) * pack, :] >= trial_b).astype(DIGIT_DTYPE)
                                           for r in range(tq // pack)))

            a = _pair_loop(nkt, cnt_body, jnp.zeros((pack, tq), DIGIT_DTYPE))
            n = jnp.sum(a.astype(jnp.int32), axis=0, keepdims=True)
            return jnp.where(n >= topk, trial, c), jnp.where(n >= topk, n, n_c)

        return lax.fori_loop(0, w, bit_body, (jnp.full((1, tq), -half, jnp.int32), n_floor))

    prefix = None
    n_ge = jnp.full((1, tq), 1, jnp.int32) * (nkt * tq)
    for stage in range(32 // w):
        shift = 32 - (stage + 1) * w

        def digit_body(kt, carry):
            if prefix is None:
                val = keys_scr[kt] >> shift
            else:
                lo = prefix << (shift + w)
                hi = lo + jnp.int32(2 ** (shift + w) - 1)
                val = ((jnp.clip(keys_scr[kt], lo, hi) - lo) >> shift) - half
            digit_scr[kt] = val.astype(DIGIT_DTYPE)
            return carry

        lax.fori_loop(0, nkt, digit_body, 0)
        digit, n_ge = kth_largest_digit(n_ge)
        prefix = digit if prefix is None else (prefix << w) + (digit + half)
    def count_scores_ge(trial):
        trial_b = jnp.broadcast_to(trial, (SUBLANES, tq))

        def cnt_body(kt, a):
            return a + _tree(jnp.add, ((sc_scr[kt, r * SUBLANES:(r + 1) * SUBLANES, :] >= trial_b).astype(jnp.int32)
                                       for r in range(tq // SUBLANES)))

        return jnp.sum(_pair_loop(nkt, cnt_body, jnp.zeros((SUBLANES, tq), jnp.int32)), axis=0, keepdims=True)

    def set_threshold(key, n_at_key):
        found = key > jnp.int32(INT_MIN)
        thr_scr[...] = jnp.where(found, _key_to_f32(key), jnp.finfo(f32).min)
        exc_scr[...] = jnp.where(jnp.logical_and(found, q_pos < seq_len), n_at_key - topk, 0)
        return found

    found = set_threshold(prefix, n_ge)
    mismatch = jnp.logical_and(found, count_scores_ge(thr_scr[...]) != n_ge)

    @pl.when(jnp.max(mismatch.astype(jnp.int32)) > 0)
    def _():
        def bit_body(it, carry):
            c, n_c = carry
            trial = c + jnp.left_shift(jnp.int32(1), 31 - it)
            n = count_scores_ge(_key_to_f32(trial))
            return jnp.where(n >= topk, trial, c), jnp.where(n >= topk, n, n_c)

        c, n_c = lax.fori_loop(0, 32, bit_body, (jnp.full((1, tq), INT_MIN, jnp.int32),
                                                 jnp.full((1, tq), 1, jnp.int32) * (nkt * tq)))
        set_threshold(c, n_c)

    thr = thr_scr[...]
    excess = exc_scr[...]
    has_ties = jnp.max(excess) > 0
    pos_bits = (keys_scr.shape[0] * tq - 1).bit_length()

    @pl.when(jnp.logical_not(has_ties))
    def _():
        def bias_body(kt, carry):
            bias_scr[kt] = jnp.where(sc_scr[kt] >= thr, 0.0, NEG_BIG)
            return carry

        lax.fori_loop(0, nkt, bias_body, 0)

    @pl.when(has_ties)
    def _():
        def tied_above(cut):
            def cnt_body(kt, a):
                tied = jnp.logical_and(sc_scr[kt] == thr, (kt * tq + k_iota) > cut)
                return a + _tree(jnp.add, (x for x in (tied.astype(jnp.int32)[r * SUBLANES:(r + 1) * SUBLANES]
                                                       for r in range(tq // SUBLANES))))

            a = lax.fori_loop(0, nkt, cnt_body, jnp.zeros((SUBLANES, tq), jnp.int32))
            return jnp.sum(a, axis=0, keepdims=True)

        def cut_body(it, cut):
            trial = cut | jnp.left_shift(jnp.int32(1), pos_bits - 1 - it)
            return jnp.where(tied_above(trial) >= excess, trial, cut)

        cut = lax.fori_loop(0, pos_bits, cut_body, jnp.zeros((1, tq), jnp.int32))

        def bias_body(kt, carry):
            key = sc_scr[kt]
            keep_tied = jnp.where((kt * tq + k_iota) <= cut, 0.0, NEG_BIG)
            bias_scr[kt] = jnp.where(key > thr, 0.0, jnp.where(key == thr, keep_tied, NEG_BIG))
            return carry

        lax.fori_loop(0, nkt, bias_body, 0)


    groups = tq // SUBLANES
    row_groups = lambda x: (x[r * SUBLANES:(r + 1) * SUBLANES] for r in range(groups))

    def logits(kt, h):
        k0 = pl.multiple_of(kt * tq, tq)
        kaug = jnp.concatenate([ka_ref[pl.ds(k0, tq), (h // 2) * LANES:(h // 2 + 1) * LANES],
                                pos_ref[pl.ds(k0, tq), :]], axis=1)
        return _dot_nt(kaug, qaug_scr[h]) + bias_scr[kt]

    def max_body(kt, m8):
        return tuple(jnp.maximum(m8[h], _tree(jnp.maximum, row_groups(logits(kt, h)))) for h in range(A_HEADS))

    m8 = _pair_loop(nkt, max_body, (jnp.full((SUBLANES, tq), NEG_BIG, f32),) * A_HEADS)

    m_cols = jnp.concatenate([jnp.max(m8[h], axis=0, keepdims=True) for h in range(A_HEADS)], axis=0).T
    for h in range(A_HEADS):
        m_col = m_cols[:, h:h + 1]
        m_hi = m_col.astype(bf16).astype(f32)
        coef = jnp.where(lane == 2, -m_hi, jnp.where(lane == 3, m_hi - m_col, alibi_coef(h)))
        qaug_scr[h, :, LANES:] = coef.astype(bf16)

    acc_scr[...] = jnp.zeros(acc_scr.shape, f32)
    ones = jnp.ones((ONES_ROWS, tq), bf16)

    def acc_body(kt, carry):
        for h in range(A_HEADS):
            pm_scr[h] = jnp.exp(logits(kt, h)).astype(bf16)
        for h in range(A_HEADS):
            rows = slice(h * (A_HEAD_DIM + ONES_ROWS), (h + 1) * (A_HEAD_DIM + ONES_ROWS))
            v_ones = jnp.concatenate([vt_ref[kt, h * A_HEAD_DIM:(h + 1) * A_HEAD_DIM, :], ones], axis=0)
            acc_scr[rows, :] += _dot(v_ones, pm_scr[h])
        return carry

    _pair_loop(nkt, acc_body, 0)
    for h in range(A_HEADS):
        r0 = h * (A_HEAD_DIM + ONES_ROWS)
        out_scr[h * A_HEAD_DIM:(h + 1) * A_HEAD_DIM, :] = (acc_scr[r0:r0 + A_HEAD_DIM, :]
                                                           / acc_scr[r0 + A_HEAD_DIM:r0 + A_HEAD_DIM + 1, :])
    o_ref[...] = out_scr[...].T.astype(o_ref.dtype)


def _dsa(qa, qi, wit, klr, ka, vt, *, topk, tq, seq_len):
    B, Lp, _ = qa.shape
    nq = Lp // tq
    assert nq * (tq // (SUBLANES * (32 // DIGIT_BITS))) < 2 ** (DIGIT_BITS - 1)
    kpos = np.arange(Lp)
    posfeat = np.zeros((Lp, LANES), np.float32)
    posfeat[:, 0] = kpos // tq
    posfeat[:, 1] = kpos % tq
    posfeat[:, 2:4] = 1.0
    posfeat = jnp.asarray(posfeat, bf16)
    qrow = lambda c: pl.BlockSpec((None, tq, c), lambda b, i: (b, i, 0))
    seq = lambda *s: pl.BlockSpec((None,) + s, lambda b, i: (b,) + (0,) * len(s))
    return pl.pallas_call(
        functools.partial(_dsa_kernel, topk=topk, tq=tq, seq_len=seq_len),
        grid=(B, nq),
        in_specs=[qrow(A_WIDTH), qrow(IDX_HEADS * IDX_DIM),
                  pl.BlockSpec((None, None, IDX_HEADS, tq), lambda b, i: (b, i, 0, 0)),
                  seq(nq, 2 * tq, LANES), seq(Lp, A_WIDTH), seq(nq, A_WIDTH, tq), _const_spec(posfeat.shape)],
        out_specs=qrow(A_WIDTH),
        out_shape=jax.ShapeDtypeStruct((B, Lp, A_WIDTH), bf16),
        scratch_shapes=[
            pltpu.VMEM((nq, tq, tq), jnp.int32),
            pltpu.VMEM((nq, tq, tq), f32),
            pltpu.VMEM((nq, tq, tq), DIGIT_DTYPE),
            pltpu.VMEM((1, tq), f32),
            pltpu.VMEM((1, tq), jnp.int32),
            pltpu.VMEM((nq, tq, tq), f32),
            pltpu.VMEM((tq, tq), f32),
            pltpu.VMEM((A_HEADS, tq, 2 * LANES), bf16),
            pltpu.VMEM((A_HEADS, tq, tq), bf16),
            pltpu.VMEM((A_HEADS * (A_HEAD_DIM + ONES_ROWS), tq), f32),
            pltpu.VMEM((A_WIDTH, tq), f32),
        ],
        compiler_params=_params("parallel", "arbitrary"),
        name="dsa",
    )(qa, qi, wit, klr, ka, vt, posfeat)


def _hgrn2_offsets(C):
    base = [HG_SUB * (t // HG_SUB) for t in range(C)]
    ns = [SUBLANES * ((t - base[t]) // SUBLANES + 1) for t in range(C)]
    return base, ns, np.concatenate([[0], np.cumsum(ns)]).tolist()


def _hgrn2_kernel(qb_ref, logf_ref, lk_ref, ib_ref, gb_ref, gain_ref, tril_ref, bones_ref, o_ref,
                  st_scr, a_scr, w_scr, intra_scr):
    C = HG_CHUNK
    P = qb_ref.shape[0]
    pairs = range(P)
    base, ns, off = _hgrn2_offsets(C)

    @pl.when(pl.program_id(1) == 0)
    def _():
        st_scr[...] = jnp.zeros(st_scr.shape, f32)

    tril = tril_ref[...]
    bones = bones_ref[...]

    def cumsum(lf):
        hi = lf.astype(bf16)
        r1 = lf - hi.astype(f32)
        mid = r1.astype(bf16)
        lo = (r1 - mid.astype(f32)).astype(bf16)
        return _dot(tril, hi) + _dot(tril, mid) + _dot(tril, lo)

    b2 = [cumsum(logf_ref[p]) * LOG2E for p in pairs]
    c2 = [b2[p] - lk_ref[p] * LOG2E for p in pairs]
    q = [qb_ref[p] for p in pairs]
    v = [ib_ref[p] for p in pairs]

    inter = [_dot_nt((q[p] * jnp.exp2(b2[p])).astype(bf16), st_scr[p].astype(bf16)) for p in pairs]
    for p in pairs:
        b_last = b2[p][C - 1:C, :]
        k_dec = jnp.exp2(b_last - c2[p]).astype(bf16)
        upd = _dot_tn(v[p].astype(bf16), k_dec)
        st_scr[p] = st_scr[p] * jnp.exp2(b_last) + upd * bones.astype(f32)

    left = lax.broadcasted_iota(jnp.int32, (1, LANES), 1) < B_KEY_DIM
    srow = lax.broadcasted_iota(jnp.int32, (C, 1), 0)
    blocks = range(1, C // HG_SUB)
    scores = {}
    for p in pairs:
        for blk in blocks:
            t0 = blk * HG_SUB
            b_r = b2[p][t0 - 1:t0, :]
            q_dec = q[p][t0:t0 + HG_SUB] * jnp.exp2(b2[p][t0:t0 + HG_SUB] - b_r)
            k_dec = jnp.exp2(jnp.where(srow < t0, b_r - c2[p], -jnp.inf)).astype(bf16)
            lhs = jnp.concatenate([jnp.where(left, q_dec, 0.0), jnp.where(left, 0.0, q_dec)], axis=0).astype(bf16)
            scores[p, blk] = _dot_nt(lhs, k_dec).astype(bf16)
    early = []
    for p in pairs:
        vb = v[p].astype(bf16)
        parts = [jnp.zeros((HG_SUB, LANES), f32)]
        for blk in blocks:
            res = _dot(scores[p, blk], vb)
            parts.append(jnp.where(left, res[:HG_SUB], res[HG_SUB:]))
        early.append(jnp.concatenate(parts, axis=0))

    sidx = lax.broadcasted_iota(jnp.int32, (SUBLANES, 1), 0)
    for p in pairs:
        for t in range(C):
            s0, n = base[t], ns[t]
            diff = b2[p][t:t + 1, :] - c2[p][s0:s0 + n]
            last = jnp.where(sidx <= t % SUBLANES, diff[n - SUBLANES:], -jnp.inf)
            diff = last if n == SUBLANES else jnp.concatenate([diff[:n - SUBLANES], last], axis=0)
            a_scr[p, off[t]:off[t] + n, :] = q[p][t:t + 1, :] * jnp.exp2(diff)
    for p in pairs:
        w_scr[p] = _dot(a_scr[p].astype(bf16), bones)
    for p in pairs:
        for t in range(C):
            s0, n = base[t], ns[t]
            intra_scr[p, t:t + 1, :] = jnp.sum(w_scr[p, off[t]:off[t] + n, :] * v[p][s0:s0 + n], axis=0,
                                               keepdims=True)

    o = [inter[p] + early[p] + intra_scr[p] for p in pairs]
    ss = []
    for p in pairs:
        o2 = o[p] * o[p]
        o2_hi = o2.astype(bf16)
        o2_lo = (o2 - o2_hi.astype(f32)).astype(bf16)
        ss.append(_dot(o2_hi, bones) + _dot(o2_lo, bones))
    for p in pairs:
        on = o[p] * lax.rsqrt(ss[p] * (1.0 / B_KEY_DIM) + LN_EPS) * gain_ref[p]
        g = gb_ref[p]
        o_ref[p] = (on * (g / (1.0 + jnp.exp(-g)))).astype(o_ref.dtype)


def _hgrn2(qb, logf, logk, ib, gb, gain):
    B, P, Lp, W = qb.shape
    C = HG_CHUNK
    head = np.arange(W) // B_KEY_DIM
    tril = jnp.asarray(np.tril(np.ones((C, C), np.float32)), bf16)
    bones = jnp.asarray((head[:, None] == head[None, :]).astype(np.float32), bf16)
    rows = _hgrn2_offsets(C)[2][-1]
    row = pl.BlockSpec((None, P, C, W), lambda b, c: (b, 0, c, 0))
    return pl.pallas_call(
        _hgrn2_kernel,
        grid=(B, Lp // C),
        in_specs=[row] * 5 + [_const_spec(gain.shape), _const_spec(tril.shape), _const_spec(bones.shape)],
        out_specs=row,
        out_shape=jax.ShapeDtypeStruct((B, P, Lp, W), bf16),
        scratch_shapes=[pltpu.VMEM((P, W, W), f32), pltpu.VMEM((P, rows, W), f32), pltpu.VMEM((P, rows, W), f32),
                        pltpu.VMEM((P, C, W), f32)],
        compiler_params=_params("parallel", "arbitrary"),
        name="hgrn2",
    )(qb, logf, logk, ib, gb, gain, tril, bones)


def _ffn_tail(h, mix, g1_ref, b1_ref, wg_ref, wu_ref, wd_ref, g2_ref, b2_ref, out_ref):
    h1 = _layer_norm(ALPHA * h + mix, g1_ref[...], b1_ref[...])
    hb = h1.astype(bf16)
    g = _dot(hb, wg_ref[...])
    u = _dot(hb, wu_ref[...])
    act = (g / (1.0 + jnp.exp(-g))) * u
    f = _dot(act.astype(bf16), wd_ref[...])
    out_ref[...] = _layer_norm(ALPHA * h1 + f, g2_ref[...], b2_ref[...])


def _post_even_kernel(h_ref, oa_ref, ob_ref, wout_ref, *rest):
    ob = jnp.concatenate([ob_ref[p] for p in range(B_WIDTH // LANES)], axis=1)
    mix = _dot(oa_ref[...], wout_ref[:A_WIDTH, :]) + _dot(ob, wout_ref[A_WIDTH:, :])
    _ffn_tail(h_ref[...], mix, *rest)


def _post_odd_kernel(h_ref, prev_ref, wpool_ref, pscale_ref, *rest, tm, skip):
    row0 = skip + pl.program_id(1) * tm
    h = h_ref[...].reshape(h_ref.shape[-2:])
    prev = jnp.where(row0 == 0, 0.0, prev_ref[...])
    x = jnp.concatenate([prev, h], axis=0)
    pos = row0 + lax.broadcasted_iota(jnp.int32, (tm, 1), 0)
    parts = []
    for g, win in enumerate(POOL_WINDOWS):
        sl = slice(g * POOL_GROUP_DIM, (g + 1) * POOL_GROUP_DIM)
        s = x[:, sl]
        span = 1
        while span < win:
            s = s + pltpu.roll(s, span, axis=0)
            span *= 2
        cnt = jnp.minimum(pos + 1, win).astype(f32)
        d = s[POOL_HALO:, :] / cnt - h[:, sl]
        parts.append(_dot(d.astype(bf16), wpool_ref[g]))
    mix = jnp.concatenate(parts, axis=-1) * pscale_ref[...]
    _ffn_tail(h, mix, *rest)


POST_TILE_MAX = 576


def _post_tile(Lp):
    return max(t for t in range(POOL_HALO, POST_TILE_MAX + 1, POOL_HALO) if Lp % t == 0)


def _post(h, mixer_inputs, mixer_specs, kernel, tail_consts, *, tm, skip=0, out_len=None):
    B, Lp, D = h.shape
    out_len = Lp if out_len is None else out_len
    row = pl.BlockSpec((None, tm, D), lambda b, i: (b, i, 0))
    h_rows = row if skip == 0 else pl.BlockSpec((pl.Element(1), pl.Element(tm), pl.Element(D)),
                                                lambda b, i: (b, pl.multiple_of(skip + i * tm, POOL_HALO), 0))
    return pl.pallas_call(
        kernel,
        grid=(B, out_len // tm),
        in_specs=[h_rows] + mixer_specs + [_const_spec(c.shape) for c in tail_consts],
        out_specs=row,
        out_shape=jax.ShapeDtypeStruct((B, out_len, D), f32),
        compiler_params=_params("parallel", "parallel"),
        name=getattr(kernel, "__name__", "post_odd"),
    )(h, *mixer_inputs, *tail_consts)


def _post_even(h, oa, ob, w_out, tail_consts, *, tm):
    row = lambda c: pl.BlockSpec((None, tm, c), lambda b, i: (b, i, 0))
    pairs = pl.BlockSpec((None, B_WIDTH // LANES, tm, LANES), lambda b, i: (b, 0, i, 0))
    return _post(h, [oa, ob, w_out], [row(A_WIDTH), pairs, _const_spec(w_out.shape)],
                 _post_even_kernel, tail_consts, tm=tm)


def _post_odd(h, w_pool, pool_scale, tail_consts, *, tm, skip=0, out_len=None):
    D = h.shape[-1]
    assert tm % POOL_HALO == 0 and skip % POOL_HALO == 0
    prev = pl.BlockSpec((None, POOL_HALO, D),
                        lambda b, i: (b, jnp.maximum((skip + i * tm) // POOL_HALO - 1, 0), 0))
    kern = functools.partial(_post_odd_kernel, tm=tm, skip=skip)
    kern.__name__ = "post_odd"
    return _post(h, [h, w_pool, pool_scale], [prev, _const_spec(w_pool.shape), _const_spec(pool_scale.shape)],
                 kern, tail_consts, tm=tm, skip=skip, out_len=out_len)


_W_IN_SPLITS = np.cumsum([0, Q_RANK, A_WIDTH, A_WIDTH, IDX_DIM, IDX_HEADS, B_WIDTH, B_WIDTH, B_WIDTH, B_WIDTH])


def _w_in_seg(w, j):
    return w[:, _W_IN_SPLITS[j]:_W_IN_SPLITS[j + 1]]


def _pack_w_in(w):
    segs = [_w_in_seg(w, j) for j in (0, 1, 3, 3, 5, 6, 7, 8)]
    return jnp.concatenate(segs, axis=1).astype(bf16)


def kernel(x, meta_tokens, w_in, cq_gain, w_uq, w_uq_idx, kidx_gain, kidx_bias, lb_raw, onorm_gain, w_out,
           w_pool, pool_scale, ln_mix_g, ln_mix_b, w_gate, w_up, w_down, ln_ffn_g, ln_ffn_b):
    B, S, D = x.shape
    L = N_META + S
    Lp = -(-L // SEQ_TILE) * SEQ_TILE
    topk = min(TOPK_MAX, S // 4)
    tm = SEQ_TILE

    meta = jnp.broadcast_to(meta_tokens.astype(x.dtype)[None], (B, N_META, D))
    h = jnp.concatenate([meta, x, jnp.zeros((B, Lp - L, D), x.dtype)], axis=1)

    lower = jnp.cumsum(jax.nn.softmax(lb_raw.astype(f32), axis=0), axis=0)
    lower = lower - lower[:1]

    row = lambda v: v.reshape(1, -1).astype(f32)
    for layer in range(DEPTH):
        j = layer // 2
        tail = [row(ln_mix_g[layer]), row(ln_mix_b[layer]), w_gate[layer].astype(bf16), w_up[layer].astype(bf16),
                w_down[layer].astype(bf16), row(ln_ffn_g[layer]), row(ln_ffn_b[layer])]
        if layer % 2 == 0:
            lb = lower[j]
            qa, ka, vt, qi, klr, wit, qb, logf, logk, ib, gb = _inproj(
                h, _pack_w_in(w_in[j]), _w_in_seg(w_in[j], 2).T.astype(bf16), _w_in_seg(w_in[j], 4).T.astype(bf16),
                row(cq_gain[j]), w_uq[j].astype(bf16), w_uq_idx[j].astype(bf16),
                row(jnp.tile(kidx_gain[j], 2)), row(jnp.tile(kidx_bias[j], 2)),
                row(jnp.log(lb)), row(jnp.log1p(-lb)), tm=tm)
            oa = _dsa(qa, qi, wit, klr, ka, vt, topk=topk, tq=SEQ_TILE, seq_len=L)
            ob = _hgrn2(qb, logf, logk, ib, gb, onorm_gain[j].astype(f32).reshape(-1, 1, LANES))
            h = _post_even(h, oa, ob, w_out[j].astype(bf16), tail, tm=_post_tile(Lp))
        elif layer < DEPTH - 1:
            h = _post_odd(h, w_pool[j].astype(bf16), row(pool_scale[j]), tail, tm=_post_tile(Lp))
        else:
            h = _post_odd(h, w_pool[j].astype(bf16), row(pool_scale[j]), tail, tm=_post_tile(S), skip=N_META, out_len=S)
    return h
```
